```python
import math
import jax
import jax.numpy as jnp
from jax import lax
import numpy as np

D_MODEL = 1024
BATCH = 16
SEQ = 256
DEPTH = 2
DEC_BATCH = 8
DEC_SEQ = 4096
PAST_LEN = 512

GRID_W = 64
HEAD_DIM = 64
A_HEADS = 4
B_HEADS = 4
B_KV_HEADS = 2
C_HEADS = 4
C_KV_HEADS = 2
D_HEADS = 4
D_HALF = HEAD_DIM // 2
CACHE_HEADS = A_HEADS + B_KV_HEADS + C_KV_HEADS + D_HEADS
MIX_WIDTH = (A_HEADS + B_HEADS + C_HEADS + D_HEADS) * HEAD_DIM
NA_ROWS = 8
NA_COLS = 16
WINDOW = 128
Q_BLOCK = 128
ROPE_BASE = 10000.0
N_GROUPS = 4
EXPERTS_PER_GROUP = 8
N_EXPERTS = N_GROUPS * EXPERTS_PER_GROUP
TOP_K_IN_GROUP = 2
EXPERT_FF = 512
EPS = 1e-6
NEG_INF = -1e30
QKV_WIDTHS = (A_HEADS * HEAD_DIM, A_HEADS * HEAD_DIM, A_HEADS * HEAD_DIM,
              B_HEADS * HEAD_DIM, B_KV_HEADS * HEAD_DIM, B_KV_HEADS * HEAD_DIM,
              C_HEADS * HEAD_DIM, C_KV_HEADS * HEAD_DIM, C_KV_HEADS * HEAD_DIM,
              D_HEADS * HEAD_DIM, D_HEADS * HEAD_DIM, D_HEADS * HEAD_DIM)
QKV_WIDTH = sum(QKV_WIDTHS)

kernel_name = 'hybrid_dit_natten_window_qknorm_diff_hmoe'


def rms_norm(x, g):
    xf = x.astype(jnp.float32)
    y = xf * lax.rsqrt(jnp.mean(xf * xf, axis=-1, keepdims=True) + EPS)
    return (y * g.astype(jnp.float32)).astype(x.dtype)


def axial_rope_tables(n, rot_dim):
    t = jnp.arange(n, dtype=jnp.int32)
    row = (t // GRID_W).astype(jnp.float32)
    col = (t % GRID_W).astype(jnp.float32)
    nf = rot_dim // 4
    inv = ROPE_BASE ** (-jnp.arange(nf, dtype=jnp.float32) / nf)
    ang = jnp.concatenate([row[:, None] * inv, col[:, None] * inv], axis=-1)
    return jnp.cos(ang)[:, None, :], jnp.sin(ang)[:, None, :]


def apply_rope(x, cos, sin):
    xp = x.astype(jnp.float32).reshape(x.shape[:-1] + (x.shape[-1] // 2, 2))
    x1, x2 = xp[..., 0], xp[..., 1]
    out = jnp.stack([x1 * cos - x2 * sin, x1 * sin + x2 * cos], axis=-1)
    return out.reshape(x.shape).astype(x.dtype)


def rope_halves(x, cos, sin):
    return jnp.concatenate([apply_rope(x[..., :D_HALF], cos, sin), apply_rope(x[..., D_HALF:], cos, sin)], axis=-1)


def joint_softmax(*scores):
    p = jax.nn.softmax(jnp.concatenate(scores, axis=-1), axis=-1)
    idx = []
    acc = 0
    for s in scores[:-1]:
        acc += s.shape[-1]
        idx.append(acc)
    return jnp.split(p, idx, axis=-1)


def dense_attn(q, k, v, sink=None):
    b, n, h, d = q.shape
    hkv = k.shape[2]
    g = h // hkv
    nb = n // Q_BLOCK
    scale = d ** -0.5
    qb = q.reshape(b, nb, Q_BLOCK, hkv, g, d).transpose(1, 0, 2, 3, 4, 5)

    def one_block(qi):
        s = jnp.einsum('bqkgd,bmkd->bkgqm', qi, k, preferred_element_type=jnp.float32) * scale
        if sink is not None:
            s_sink = jnp.broadcast_to(sink.astype(jnp.float32).reshape(1, hkv, g, 1, 1), s.shape[:-1] + (1,))
            p, _ = joint_softmax(s, s_sink)
        else:
            p = jax.nn.softmax(s, axis=-1)
        return jnp.einsum('bkgqm,bmkd->bqkgd', p.astype(v.dtype), v)

    o = lax.map(one_block, qb)
    return o.transpose(1, 0, 2, 3, 4, 5).reshape(b, n, h, v.shape[-1])


def natten_attn(q, k, v, k_ctx, v_ctx, rpb):
    b, n, h, d = q.shape
    rows = n // GRID_W
    kh = min(NA_ROWS, rows)
    kw = NA_COLS
    scale = d ** -0.5
    r = jnp.arange(rows)
    c = jnp.arange(GRID_W)
    row_idx = jnp.clip(r - kh // 2, 0, rows - kh)[:, None] + jnp.arange(kh)[None, :]
    col_start = jnp.clip(c - kw // 2, 0, GRID_W - kw)
    col_ok = (c[None, :] >= col_start[:, None]) & (c[None, :] < col_start[:, None] + kw)
    q5 = q.reshape(b, rows, GRID_W, h, d)
    kg = jnp.take(k.reshape(b, rows, GRID_W, h, d), row_idx, axis=1)
    vg = jnp.take(v.reshape(b, rows, GRID_W, h, d), row_idx, axis=1)
    s_loc = jnp.einsum('brqhd,brjwhd->bhrqjw', q5, kg, preferred_element_type=jnp.float32) * scale
    d_row = row_idx - r[:, None] + (NA_ROWS - 1)
    d_col = jnp.clip(c[None, :] - c[:, None], -(kw - 1), kw - 1) + (NA_COLS - 1)
    bias = rpb[:, d_row[:, None, :, None], d_col[None, :, None, :]]
    s_loc = jnp.where(col_ok[:, None, :], s_loc + bias.astype(jnp.float32)[None], NEG_INF)
    s_loc = s_loc.reshape(b, h, rows, GRID_W, kh * GRID_W)
    s_ctx = jnp.einsum('brqhd,bmhd->bhrqm', q5, k_ctx, preferred_element_type=jnp.float32) * scale
    p_loc, p_ctx = joint_softmax(s_loc, s_ctx)
    p_loc = p_loc.reshape(b, h, rows, GRID_W, kh, GRID_W).astype(v.dtype)
    o = (jnp.einsum('bhrqjw,brjwhd->brqhd', p_loc, vg)
         + jnp.einsum('bhrqm,bmhd->brqhd', p_ctx.astype(v.dtype), v_ctx))
    return o.reshape(b, n, h, d)


def window_attn(q, k, v, k_ctx, v_ctx, sink):
    b, n, h, d = q.shape
    hkv = k.shape[2]
    g = h // hkv
    nb = n // Q_BLOCK
    scale = d ** -0.5
    qb = q.reshape(b, nb, Q_BLOCK, hkv, g, d)

    def band(x):
        xp = jnp.pad(x, ((0, 0), (Q_BLOCK, Q_BLOCK), (0, 0), (0, 0))).reshape(b, nb + 2, Q_BLOCK, hkv, x.shape[-1])
        return jnp.concatenate([xp[:, :-2], xp[:, 1:-1], xp[:, 2:]], axis=2)

    kband = band(k)
    vband = band(v)
    q_pos = jnp.arange(n).reshape(nb, Q_BLOCK)
    k_pos = (jnp.arange(nb)[:, None] - 1) * Q_BLOCK + jnp.arange(3 * Q_BLOCK)[None, :]
    kp = k_pos[:, None, :]
    ok = (kp >= 0) & (kp < n) & (jnp.abs(kp - q_pos[:, :, None]) <= WINDOW)
    s_loc = jnp.einsum('bnqkgd,bnjkd->bkgnqj', qb, kband, preferred_element_type=jnp.float32) * scale
    s_loc = jnp.where(ok, s_loc, NEG_INF)
    s_ctx = jnp.einsum('bnqkgd,bmkd->bkgnqm', qb, k_ctx, preferred_element_type=jnp.float32) * scale
    s_sink = jnp.broadcast_to(sink.astype(jnp.float32).reshape(1, hkv, g, 1, 1, 1), s_loc.shape[:-1] + (1,))
    p_loc, p_ctx, _ = joint_softmax(s_loc, s_ctx, s_sink)
    o = (jnp.einsum('bkgnqj,bnjkd->bnqkgd', p_loc.astype(v.dtype), vband)
         + jnp.einsum('bkgnqm,bmkd->bnqkgd', p_ctx.astype(v.dtype), v_ctx))
    return o.reshape(b, n, h, d)


def diff_attn(q, k, v, lam, lam_init, sub_g):
    a1 = dense_attn(q[..., :D_HALF], k[..., :D_HALF], v)
    a2 = dense_attn(q[..., D_HALF:], k[..., D_HALF:], v)
    return (rms_norm(a1 - lam * a2, sub_g) * (1.0 - lam_init)).astype(v.dtype)


def diff_lambda(lq1, lk1, lq2, lk2, lam_init):
    return (jnp.exp(jnp.sum(lq1 * lk1).astype(jnp.float32))
            - jnp.exp(jnp.sum(lq2 * lk2).astype(jnp.float32)) + lam_init)


def split_qkv(h, w_in):
    b, n, _ = h.shape
    qkv = h @ w_in
    points = []
    acc = 0
    for wd in QKV_WIDTHS[:-1]:
        acc += wd
        points.append(acc)
    return [p.reshape(b, n, -1, HEAD_DIM) for p in jnp.split(qkv, points, axis=-1)]


def modulation(cond, w_ada, b_ada):
    return jnp.split(jax.nn.silu(cond) @ w_ada + b_ada, 6, axis=-1)


def attn_context(h, w_in, w_out, sink, q_g, k_g, lam, lam_init, sub_g):
    b, n, _ = h.shape
    qa, ka, va, qb, kb, vb, qc, kc, vc, qd, kd, vd = split_qkv(h, w_in)
    qc = rms_norm(qc, q_g)
    kc = rms_norm(kc, k_g)
    oa = dense_attn(qa, ka, va)
    ob = dense_attn(qb, kb, vb, sink)
    oc = dense_attn(qc, kc, vc)
    od = diff_attn(qd, kd, vd, lam, lam_init, sub_g)
    o = jnp.concatenate([oa, ob, oc, od], axis=2).reshape(b, n, MIX_WIDTH) @ w_out
    k_ctx = jnp.concatenate([ka, kb, kc, kd], axis=2)
    v_ctx = jnp.concatenate([va, vb, vc, vd], axis=2)
    return o, k_ctx, v_ctx


def attn_latent(h, k_ctx, v_ctx, w_in, w_out, rpb, sink, q_g, k_g, lam, lam_init, sub_g):
    b, n, _ = h.shape
    qa, ka, va, qb, kb, vb, qc, kc, vc, qd, kd, vd = split_qkv(h, w_in)
    cos, sin = axial_rope_tables(n, HEAD_DIM)
    cos_d, sin_d = axial_rope_tables(n, D_HALF)
    a_end = A_HEADS
    b_end = a_end + B_KV_HEADS
    c_end = b_end + C_KV_HEADS
    ka_c, kb_c, kc_c, kd_c = k_ctx[:, :, :a_end], k_ctx[:, :, a_end:b_end], k_ctx[:, :, b_end:c_end], k_ctx[:, :, c_end:]
    va_c, vb_c, vc_c, vd_c = v_ctx[:, :, :a_end], v_ctx[:, :, a_end:b_end], v_ctx[:, :, b_end:c_end], v_ctx[:, :, c_end:]
    oa = natten_attn(qa, ka, va, ka_c, va_c, rpb)
    ob = window_attn(apply_rope(qb, cos, sin), apply_rope(kb, cos, sin), vb, kb_c, vb_c, sink)
    qc = apply_rope(rms_norm(qc, q_g), cos, sin)
    kc = apply_rope(rms_norm(kc, k_g), cos, sin)
    oc = dense_attn(qc, jnp.concatenate([kc, kc_c], axis=1), jnp.concatenate([vc, vc_c], axis=1))
    od = diff_attn(rope_halves(qd, cos_d, sin_d),
                   jnp.concatenate([rope_halves(kd, cos_d, sin_d), kd_c], axis=1),
                   jnp.concatenate([vd, vd_c], axis=1), lam, lam_init, sub_g)
    return jnp.concatenate([oa, ob, oc, od], axis=2).reshape(b, n, MIX_WIDTH) @ w_out


def hier_moe(x, w_rg, w_re, w_gate, w_up, w_down):
    shp = x.shape
    t = x.reshape(-1, shp[-1])
    pg = jax.nn.softmax(jnp.einsum('td,dg->tg', t, w_rg, preferred_element_type=jnp.float32), axis=-1)
    pg_top, g_sel = lax.top_k(pg, 1)
    le = jnp.einsum('td,de->te', t, w_re, preferred_element_type=jnp.float32).reshape(-1, N_GROUPS, EXPERTS_PER_GROUP)
    le_sel = jnp.take_along_axis(le, g_sel[:, :, None], axis=1)[:, 0]
    top_v, top_i = lax.top_k(le_sel, TOP_K_IN_GROUP)
    w_top = jax.nn.softmax(top_v, axis=-1) * pg_top
    expert_id = g_sel * EXPERTS_PER_GROUP + top_i
    combine = jnp.sum(jax.nn.one_hot(expert_id, N_EXPERTS, dtype=jnp.float32) * w_top[..., None], axis=1).astype(t.dtype)
    y = jnp.zeros_like(t)
    for e in range(N_EXPERTS):
        hdn = jax.nn.silu(t @ w_gate[e]) * (t @ w_up[e])
        y = y + combine[:, e:e + 1] * (hdn @ w_down[e])
    return y.reshape(shp)


def setup_inputs(seed: int = 0) -> dict:
    key = jax.random.key(seed)
    ks = jax.random.split(key, 28)

    def nrm(k, shape, scale):
        return jax.random.normal(k, shape, jnp.float32) * scale

    def gain(k, shape):
        return 1.0 + nrm(k, shape, 0.01)

    return {
        'x_prompt': nrm(ks[0], (BATCH, SEQ, D_MODEL), 1.0),
        'x_sample': nrm(ks[1], (DEC_BATCH, DEC_SEQ, D_MODEL), 1.0),
        'cache_k': nrm(ks[2], (DEC_BATCH, DEPTH, PAST_LEN, CACHE_HEADS, HEAD_DIM), 1.0),
        'cache_v': nrm(ks[3], (DEC_BATCH, DEPTH, PAST_LEN, CACHE_HEADS, HEAD_DIM), 1.0),
        'c': nrm(ks[4], (DEC_BATCH, D_MODEL), 1.0),
        'c_ctx': nrm(ks[5], (D_MODEL,), 1.0),
        'w_ada': nrm(ks[6], (DEPTH, D_MODEL, 6 * D_MODEL), 0.5 * D_MODEL ** -0.5),
        'b_ada': nrm(ks[7], (DEPTH, 6 * D_MODEL), 0.02),
        'norm1_g': gain(ks[8], (DEPTH, D_MODEL)),
        'norm2_g': gain(ks[9], (DEPTH, D_MODEL)),
        'w_in': nrm(ks[10], (DEPTH, D_MODEL, QKV_WIDTH), D_MODEL ** -0.5),
        'w_out': nrm(ks[11], (DEPTH, MIX_WIDTH, D_MODEL), MIX_WIDTH ** -0.5),
        'na_rpb': nrm(ks[12], (DEPTH, A_HEADS, 2 * NA_ROWS - 1, 2 * NA_COLS - 1), 0.1),
        'sink_logit': nrm(ks[13], (DEPTH, B_HEADS), 0.5),
        'qn_g': gain(ks[14], (DEPTH, HEAD_DIM)),
        'kn_g': gain(ks[15], (DEPTH, HEAD_DIM)),
        'lam_q1': nrm(ks[16], (DEPTH, D_HALF), 0.1),
        'lam_k1': nrm(ks[17], (DEPTH, D_HALF), 0.1),
        'lam_q2': nrm(ks[18], (DEPTH, D_HALF), 0.1),
        'lam_k2': nrm(ks[19], (DEPTH, D_HALF), 0.1),
        'subln_g': gain(ks[20], (DEPTH, HEAD_DIM)),
        'w_router_group': nrm(ks[21], (DEPTH, D_MODEL, N_GROUPS), D_MODEL ** -0.5),
        'w_router_expert': nrm(ks[22], (DEPTH, D_MODEL, N_EXPERTS), D_MODEL ** -0.5),
        'w_gate': nrm(ks[23], (DEPTH, N_EXPERTS, D_MODEL, EXPERT_FF), D_MODEL ** -0.5),
        'w_up': nrm(ks[24], (DEPTH, N_EXPERTS, D_MODEL, EXPERT_FF), D_MODEL ** -0.5),
        'w_down': nrm(ks[25], (DEPTH, N_EXPERTS, EXPERT_FF, D_MODEL), EXPERT_FF ** -0.5),
        'final_g': gain(ks[26], (D_MODEL,)),
    }


def reference(x_prompt, x_sample, cache_k, cache_v, c, c_ctx, w_ada, b_ada, norm1_g, norm2_g, w_in, w_out,
              na_rpb, sink_logit, qn_g, kn_g, lam_q1, lam_k1, lam_q2, lam_k2, subln_g,
              w_router_group, w_router_expert, w_gate, w_up, w_down, final_g):
    xp = x_prompt
    xs = x_sample
    ctx_cond = c_ctx[None, None, :]
    lat_cond = c[:, None, :]
    new_k = []
    new_v = []
    for l in range(DEPTH):
        lam_init = 0.8 - 0.6 * math.exp(-0.3 * l)
        lam = diff_lambda(lam_q1[l], lam_k1[l], lam_q2[l], lam_k2[l], lam_init)
        sh1, sc1, g1, sh2, sc2, g2 = modulation(ctx_cond, w_ada[l], b_ada[l])
        h = rms_norm(xp, norm1_g[l]) * (1.0 + sc1) + sh1
        o, k_ctx, v_ctx = attn_context(h, w_in[l], w_out[l], sink_logit[l], qn_g[l], kn_g[l], lam, lam_init, subln_g[l])
        xp = xp + g1 * o
        h = rms_norm(xp, norm2_g[l]) * (1.0 + sc2) + sh2
        xp = xp + g2 * hier_moe(h, w_router_group[l], w_router_expert[l], w_gate[l], w_up[l], w_down[l])
        new_k.append(k_ctx)
        new_v.append(v_ctx)
        sh1, sc1, g1, sh2, sc2, g2 = modulation(lat_cond, w_ada[l], b_ada[l])
        h = rms_norm(xs, norm1_g[l]) * (1.0 + sc1) + sh1
        o = attn_latent(h, cache_k[:, l], cache_v[:, l], w_in[l], w_out[l], na_rpb[l], sink_logit[l],
                        qn_g[l], kn_g[l], lam, lam_init, subln_g[l])
        xs = xs + g1 * o
        h = rms_norm(xs, norm2_g[l]) * (1.0 + sc2) + sh2
        xs = xs + g2 * hier_moe(h, w_router_group[l], w_router_expert[l], w_gate[l], w_up[l], w_down[l])
    y_prompt = rms_norm(xp, final_g)
    y_sample = rms_norm(xs, final_g)
    new_cache_k = jnp.stack(new_k, axis=1)
    new_cache_v = jnp.stack(new_v, axis=1)
    return (y_prompt, y_sample, new_cache_k, new_cache_v)
```

```python
import functools
import math

import numpy as np
import jax
import jax.numpy as jnp
from jax import lax
from jax.experimental import pallas as pl
from jax.experimental.pallas import tpu as pltpu

D_MODEL = 1024
DEPTH = 2
GRID_W = 64
HEAD_DIM = 64
D_HALF = HEAD_DIM // 2
NA_ROWS = 8
NA_COLS = 16
WINDOW = 128
ROPE_BASE = 10000.0
N_GROUPS = 4
EXPERTS_PER_GROUP = 8
N_EXPERTS = N_GROUPS * EXPERTS_PER_GROUP
EXPERT_FF = 512
EPS = 1e-6
NEG_INF = -1e30
QKV_WIDTH = 2560
MIX_WIDTH = 1024
CACHE_WIDTH = 768

LANES = 128
MOE_TILE = 256
VMEM_LIMIT = 48 * 1024 * 1024

F32 = jnp.float32
BF16 = jnp.bfloat16

_GQA_ORDER = (0, 2, 1, 3)


def _qkv_perm():
    cols = np.arange(QKV_WIDTH)
    for base in (768, 1280):
        blk = cols[base:base + 256].reshape(4, HEAD_DIM)
        cols[base:base + 256] = blk[list(_GQA_ORDER)].reshape(-1)
    return cols


def _mix_perm():
    rows = np.arange(MIX_WIDTH)
    for base in (256, 512):
        blk = rows[base:base + 256].reshape(4, HEAD_DIM)
        rows[base:base + 256] = blk[list(_GQA_ORDER)].reshape(-1)
    return rows


def _lane(shape):
    return lax.broadcasted_iota(jnp.int32, shape, len(shape) - 1)


def _head_rms(x):
    lo = _lane(x.shape) < HEAD_DIM
    sq = x * x
    s_lo = jnp.sum(jnp.where(lo, sq, 0.0), axis=-1, keepdims=True)
    s_hi = jnp.sum(jnp.where(lo, 0.0, sq), axis=-1, keepdims=True)
    return jnp.where(lo, lax.rsqrt(s_lo * (1.0 / HEAD_DIM) + EPS), lax.rsqrt(s_hi * (1.0 / HEAD_DIM) + EPS))


def _swap_pairs(x):
    n = x.shape[-1]
    even = (_lane(x.shape) % 2) == 0
    return jnp.where(even, pltpu.roll(x, n - 1, axis=x.ndim - 1), pltpu.roll(x, 1, axis=x.ndim - 1))


def _qk(qm, k):
    return lax.dot_general(qm, k, (((1,), (1,)), ((), ())), preferred_element_type=F32)


def _mask_q(q, lo, hi):
    lane = _lane(q.shape)
    return jnp.where((lane >= lo) & (lane < hi), q, jnp.zeros_like(q))


def _joint_softmax_pv(scores, values, extra_logit=None):
    m = scores[0].max(axis=-1, keepdims=True)
    for s in scores[1:]:
        m = jnp.maximum(m, s.max(axis=-1, keepdims=True))
    if extra_logit is not None:
        m = jnp.maximum(m, extra_logit)
    l = None
    o = None
    for s, v in zip(scores, values):
        p = jnp.exp(s - m)
        ls = p.sum(axis=-1, keepdims=True)
        os_ = jnp.dot(p.astype(BF16), v, preferred_element_type=F32)
        l = ls if l is None else l + ls
        o = os_ if o is None else o + os_
    if extra_logit is not None:
        l = l + jnp.exp(extra_logit - m)
    return o / l


def _diff_finish(a1_h0, a2_h0, a1_h1, a2_h1, lam, subg, post_scale):
    lo = _lane(a1_h0.shape) < HEAD_DIM
    d = jnp.where(lo, a1_h0 - lam * a2_h0, a1_h1 - lam * a2_h1)
    return d * _head_rms(d) * subg * post_scale


def _mod_kernel(c_ref, w_ref, b_ref, o_ref):
    c = c_ref[...]
    s = c * (1.0 / (1.0 + jnp.exp(-c)))
    o_ref[0] = jnp.dot(s.astype(BF16), w_ref[0].astype(BF16), preferred_element_type=F32) + b_ref[0]


def _modulation(cond, w_ada, b_ada):
    r = cond.shape[0]
    tn = 1536
    return pl.pallas_call(
        _mod_kernel,
        grid=(DEPTH, 6 * D_MODEL // tn),
        in_specs=[pl.BlockSpec((r, D_MODEL), lambda l, j: (0, 0)),
                  pl.BlockSpec((1, D_MODEL, tn), lambda l, j: (l, 0, j)),
                  pl.BlockSpec((1, 1, tn), lambda l, j: (l, 0, j))],
        out_specs=pl.BlockSpec((1, r, tn), lambda l, j: (l, 0, j)),
        out_shape=jax.ShapeDtypeStruct((DEPTH, r, 6 * D_MODEL), F32),
        compiler_params=pltpu.CompilerParams(vmem_limit_bytes=VMEM_LIMIT),
        name="modulation",
    )(cond, w_ada, b_ada.reshape(DEPTH, 1, 6 * D_MODEL))


_SCALE64 = HEAD_DIM ** -0.5
_SCALE32 = D_HALF ** -0.5
_BLOCK_KIND = {0: (None, None, _SCALE64), 1: (None, None, _SCALE64),
               6: (None, 64, _SCALE64), 7: (None, 64, _SCALE64), 8: (None, 64, 1.0),
               10: (0, 64, _SCALE64), 11: (0, 64, _SCALE64), 12: (1, 64, 1.0),
               14: (None, 32, _SCALE32), 15: (None, 32, _SCALE32), 16: (None, 32, 1.0), 17: (None, 32, 1.0)}


def _qkv_kernel(x_ref, sc_ref, sh_ref, g_ref, w_ref, gains_ref, *rest, rope):
    if rope:
        cos64_ref, sin64_ref, cos32_ref, sin32_ref, o_ref = rest
    else:
        (o_ref,) = rest
    x = x_ref[0]
    ms = jnp.mean(x * x, axis=-1, keepdims=True)
    h = (x * lax.rsqrt(ms + EPS) * g_ref[...]) * (1.0 + sc_ref[0]) + sh_ref[0]
    hb = h.astype(BF16)
    for c in range(QKV_WIDTH // 256):
        acc = jnp.dot(hb, w_ref[:, c * 256:(c + 1) * 256], preferred_element_type=F32)
        for half in range(2):
            j = 2 * c + half
            y = acc[:, half * LANES:(half + 1) * LANES]
            gain_row, rope_kind, scale = _BLOCK_KIND.get(j, (None, None, 1.0))
            if gain_row is not None:
                y = y * _head_rms(y) * gains_ref[gain_row:gain_row + 1, :]
            if rope and rope_kind is not None:
                cos_ref, sin_ref = (cos64_ref, sin64_ref) if rope_kind == 64 else (cos32_ref, sin32_ref)
                y = y * cos_ref[...] + _swap_pairs(y) * sin_ref[...]
            if scale != 1.0:
                y = y * scale
            o_ref[0, :, j * LANES:(j + 1) * LANES] = y.astype(BF16)


def _qkv_proj(x, sc, sh, g, w, gains, rope_tabs, tn=256):
    b, n, _ = x.shape
    rope = rope_tabs is not None
    in_specs = [pl.BlockSpec((1, tn, D_MODEL), lambda bi, i: (bi, i, 0)),
                pl.BlockSpec((1, 1, D_MODEL), lambda bi, i: (bi, 0, 0)),
                pl.BlockSpec((1, 1, D_MODEL), lambda bi, i: (bi, 0, 0)),
                pl.BlockSpec((1, D_MODEL), lambda bi, i: (0, 0)),
                pl.BlockSpec((D_MODEL, QKV_WIDTH), lambda bi, i: (0, 0)),
                pl.BlockSpec((2, LANES), lambda bi, i: (0, 0))]
    args = [x, sc, sh, g, w, gains]
    if rope:
        in_specs += [pl.BlockSpec((tn, LANES), lambda bi, i: (i, 0))] * 4
        args += list(rope_tabs)
    return pl.pallas_call(
        functools.partial(_qkv_kernel, rope=rope),
        grid=(b, n // tn),
        in_specs=in_specs,
        out_specs=pl.BlockSpec((1, tn, QKV_WIDTH), lambda bi, i: (bi, i, 0)),
        out_shape=jax.ShapeDtypeStruct((b, n, QKV_WIDTH), BF16),
        compiler_params=pltpu.CompilerParams(vmem_limit_bytes=VMEM_LIMIT),
        name="qkv_rope" if rope else "qkv_ctx",
    )(*args)


def _rope_tables(n):
    t = jnp.arange(n, dtype=jnp.int32)
    row = (t // GRID_W).astype(F32)
    col = (t % GRID_W).astype(F32)
    out = []
    for rot_dim in (HEAD_DIM, D_HALF):
        nf = rot_dim // 4
        inv = ROPE_BASE ** (-jnp.arange(nf, dtype=F32) / nf)
        ang = jnp.concatenate([row[:, None] * inv, col[:, None] * inv], axis=-1)
        cos = jnp.repeat(jnp.cos(ang), 2, axis=-1)
        sin = jnp.repeat(jnp.sin(ang), 2, axis=-1)
        sign = jnp.asarray(np.tile(np.array([-1.0, 1.0], np.float32), rot_dim // 2))
        reps = LANES // rot_dim
        out += [jnp.tile(cos, (1, reps)), jnp.tile(sin * sign, (1, reps))]
    return tuple(out)


def _ctx_attn_kernel(sink_ref, lam_ref, qkv_ref, subg_ref, o_ref, *, post_scale):
    def blk(j):
        return qkv_ref[0, :, j * LANES:(j + 1) * LANES]

    def two_head(qj, kj, vj, sinks):
        q, k, v = blk(qj), blk(kj), blk(vj)
        outs = []
        for mi in range(2):
            s = _qk(_mask_q(q, mi * HEAD_DIM, (mi + 1) * HEAD_DIM), k)
            outs.append(_joint_softmax_pv([s], [v], None if sinks is None else sinks[mi]))
        return jnp.where(_lane(outs[0].shape) < HEAD_DIM, outs[0], outs[1])

    for b2 in range(2):
        o_ref[0, :, b2 * LANES:(b2 + 1) * LANES] = two_head(b2, 2 + b2, 4 + b2, None).astype(BF16)
        o_ref[0, :, (2 + b2) * LANES:(3 + b2) * LANES] = two_head(
            6 + b2, 8, 9, (sink_ref[b2], sink_ref[b2 + 2])).astype(BF16)
        o_ref[0, :, (4 + b2) * LANES:(5 + b2) * LANES] = two_head(10 + b2, 12, 13, None).astype(BF16)
        q, k, v = blk(14 + b2), blk(16 + b2), blk(18 + b2)
        a = [_joint_softmax_pv([_qk(_mask_q(q, mi * D_HALF, (mi + 1) * D_HALF), k)], [v]) for mi in range(4)]
        o_ref[0, :, (6 + b2) * LANES:(7 + b2) * LANES] = _diff_finish(
            a[0], a[1], a[2], a[3], lam_ref[0], subg_ref[...], post_scale).astype(BF16)


def _ctx_attention(qkv, sink, lam, subg, post_scale):
    b, n, _ = qkv.shape
    return pl.pallas_call(
        functools.partial(_ctx_attn_kernel, post_scale=post_scale),
        grid=(b,),
        in_specs=[pl.BlockSpec(memory_space=pltpu.SMEM),
                  pl.BlockSpec(memory_space=pltpu.SMEM),
                  pl.BlockSpec((1, n, QKV_WIDTH), lambda bi: (bi, 0, 0)),
                  pl.BlockSpec((1, LANES), lambda bi: (0, 0))],
        out_specs=pl.BlockSpec((1, n, MIX_WIDTH), lambda bi: (bi, 0, 0)),
        out_shape=jax.ShapeDtypeStruct((b, n, MIX_WIDTH), BF16),
        compiler_params=pltpu.CompilerParams(vmem_limit_bytes=VMEM_LIMIT),
        name="ctx_attention",
    )(sink, lam, qkv, subg)


_NA_QROWS = 2
_NA_KBLOCKS = 5
_NA_CASES = (0, 2, 4, 60, 62)


def _natten_bias(rpb, rows):
    assert rows == 64
    nblk = rows // _NA_QROWS
    r0s = np.array(_NA_CASES)
    nq, nk = _NA_QROWS * GRID_W, _NA_KBLOCKS * LANES
    qr = r0s[:, None] + (np.arange(nq) // GRID_W)[None, :]
    qc = np.arange(nq) % GRID_W
    start = np.clip((r0s - NA_ROWS // 2) // 2, 0, nblk - _NA_KBLOCKS)
    kr = 2 * start[:, None] + (np.arange(nk) // GRID_W)[None, :]
    kc = np.arange(nk) % GRID_W
    rs = np.clip(qr - NA_ROWS // 2, 0, rows - NA_ROWS)
    vrow = (kr[:, None, :] >= rs[:, :, None]) & (kr[:, None, :] < rs[:, :, None] + NA_ROWS)
    cs = np.clip(qc - NA_COLS // 2, 0, GRID_W - NA_COLS)
    vcol = (kc[None, :] >= cs[:, None]) & (kc[None, :] < cs[:, None] + NA_COLS)
    valid = vrow & vcol[None]
    d_row = np.clip(kr[:, None, :] - qr[:, :, None] + (NA_ROWS - 1), 0, 2 * NA_ROWS - 2)
    d_col = np.clip(kc[None, :] - qc[:, None], -(NA_COLS - 1), NA_COLS - 1) + (NA_COLS - 1)
    bias = rpb[:, d_row, np.broadcast_to(d_col[None], d_row.shape)]
    return jnp.where(valid[None], bias.astype(F32), NEG_INF)


def _natten_kernel(q_ref, k0, k1, k2, k3, k4, v0, v1, v2, v3, v4, kc_ref, vc_ref, bias_ref, o_ref):
    q = q_ref[0]
    k = jnp.concatenate([r[0] for r in (k0, k1, k2, k3, k4)], axis=0)
    v = jnp.concatenate([r[0] for r in (v0, v1, v2, v3, v4)], axis=0)
    kc = kc_ref[0, 0].astype(BF16)
    vc = vc_ref[0, 0].astype(BF16)
    outs = []
    for mi in range(2):
        qm = _mask_q(q, mi * HEAD_DIM, (mi + 1) * HEAD_DIM)
        s_loc = _qk(qm, k) + bias_ref[mi, 0]
        s_ctx = _qk(qm, kc)
        outs.append(_joint_softmax_pv([s_ctx, s_loc], [vc, v]))
    o_ref[0] = jnp.where(_lane(outs[0].shape) < HEAD_DIM, outs[0], outs[1]).astype(BF16)


def _natten(qkv, cache_k, cache_v, layer, bias):
    b, n, _ = qkv.shape
    nblk = n // LANES

    def start(i):
        return jnp.clip(i - 2, 0, nblk - _NA_KBLOCKS)

    def case(i):
        return jnp.where(i == 0, 0, jnp.where(i == 1, 1, jnp.where(i == nblk - 2, 3, jnp.where(i == nblk - 1, 4, 2))))

    def kv_spec(colblk, jj):
        return pl.BlockSpec((1, LANES, LANES), lambda bi, j, i: (bi, start(i) + jj, colblk + j))

    in_specs = ([pl.BlockSpec((1, LANES, LANES), lambda bi, j, i: (bi, i, j))]
                + [kv_spec(2, jj) for jj in range(_NA_KBLOCKS)]
                + [kv_spec(4, jj) for jj in range(_NA_KBLOCKS)]
                + [pl.BlockSpec((1, 1, cache_k.shape[2], LANES), lambda bi, j, i: (bi, layer, 0, j))] * 2
                + [pl.BlockSpec((2, 1, LANES, _NA_KBLOCKS * LANES), lambda bi, j, i: (j, case(i), 0, 0))])
    return pl.pallas_call(
        _natten_kernel,
        grid=(b, 2, nblk),
        in_specs=in_specs,
        out_specs=pl.BlockSpec((1, LANES, LANES), lambda bi, j, i: (bi, i, j)),
        out_shape=jax.ShapeDtypeStruct((b, n, 2 * LANES), BF16),
        compiler_params=pltpu.CompilerParams(vmem_limit_bytes=VMEM_LIMIT),
        name="natten",
    )(qkv, *([qkv] * (2 * _NA_KBLOCKS)), cache_k, cache_v, bias)


def _window_kernel(sink_ref, q_ref, k0, k1, k2, v0, v1, v2, kc_ref, vc_ref, o_ref, *, n):
    j = pl.program_id(1)
    i = pl.program_id(2)
    q = q_ref[0]
    k = jnp.concatenate([r[0] for r in (k0, k1, k2)], axis=0)
    v = jnp.concatenate([r[0] for r in (v0, v1, v2)], axis=0)
    kc = kc_ref[0, 0].astype(BF16)
    vc = vc_ref[0, 0].astype(BF16)
    r = lax.broadcasted_iota(jnp.int32, (LANES, 3 * LANES), 0)
    c = lax.broadcasted_iota(jnp.int32, (LANES, 3 * LANES), 1)
    kp = (i - 1) * LANES + c
    ok = (kp >= 0) & (kp < n) & (jnp.abs(c - LANES - r) <= WINDOW)
    outs = []
    for mi in range(2):
        qm = _mask_q(q, mi * HEAD_DIM, (mi + 1) * HEAD_DIM)
        s_loc = jnp.where(ok, _qk(qm, k), NEG_INF)
        s_ctx = _qk(qm, kc)
        outs.append(_joint_softmax_pv([s_ctx, s_loc], [vc, v], sink_ref[j + 2 * mi]))
    o_ref[0] = jnp.where(_lane(outs[0].shape) < HEAD_DIM, outs[0], outs[1]).astype(BF16)


def _window(qkv, cache_k, cache_v, layer, sink):
    b, n, _ = qkv.shape
    nblk = n // LANES

    def kv_spec(colblk, jj):
        return pl.BlockSpec((1, LANES, LANES), lambda bi, j, i: (bi, jnp.clip(i - 1 + jj, 0, nblk - 1), colblk))

    in_specs = ([pl.BlockSpec(memory_space=pltpu.SMEM),
                 pl.BlockSpec((1, LANES, LANES), lambda bi, j, i: (bi, i, 6 + j))]
                + [kv_spec(8, jj) for jj in range(3)]
                + [kv_spec(9, jj) for jj in range(3)]
                + [pl.BlockSpec((1, 1, cache_k.shape[2], LANES), lambda bi, j, i: (bi, layer, 0, 2))] * 2)
    return pl.pallas_call(
        functools.partial(_window_kernel, n=n),
        grid=(b, 2, nblk),
        in_specs=in_specs,
        out_specs=pl.BlockSpec((1, LANES, LANES), lambda bi, j, i: (bi, i, j)),
        out_shape=jax.ShapeDtypeStruct((b, n, 2 * LANES), BF16),
        compiler_params=pltpu.CompilerParams(vmem_limit_bytes=VMEM_LIMIT),
        name="window",
    )(sink, qkv, *([qkv] * 6), cache_k, cache_v)


def _dense_kernel(lam_ref, q_ref, k_ref, v_ref, kc_ref, vc_ref, subg_ref, o_ref, m_ref, l_ref, acc_ref,
                  *, n_maps, tk, diff, post_scale):
    q = q_ref[0]
    width = LANES // n_maps
    qms = [_mask_q(q, mi * width, (mi + 1) * width) for mi in range(n_maps)]
    kc = kc_ref[0, 0].astype(BF16)
    vc = vc_ref[0, 0].astype(BF16)
    for mi in range(n_maps):
        s = _qk(qms[mi], kc)
        m = s.max(axis=-1, keepdims=True)
        p = jnp.exp(s - m)
        m_ref[mi] = m
        l_ref[mi] = p.sum(axis=-1, keepdims=True)
        acc_ref[mi] = jnp.dot(p.astype(BF16), vc, preferred_element_type=F32)

    def body(c, carry):
        k = k_ref[0, pl.ds(pl.multiple_of(c * tk, tk), tk), :]
        v = v_ref[0, pl.ds(pl.multiple_of(c * tk, tk), tk), :]
        for mi in range(n_maps):
            s = _qk(qms[mi], k)
            m_old = m_ref[mi]
            m_new = jnp.maximum(m_old, s.max(axis=-1, keepdims=True))
            alpha = jnp.exp(m_old - m_new)
            p = jnp.exp(s - m_new)
            l_ref[mi] = alpha * l_ref[mi] + p.sum(axis=-1, keepdims=True)
            acc_ref[mi] = alpha * acc_ref[mi] + jnp.dot(p.astype(BF16), v, preferred_element_type=F32)
            m_ref[mi] = m_new
        return carry

    lax.fori_loop(0, k_ref.shape[1] // tk, body, 0)
    a = [acc_ref[mi] / l_ref[mi] for mi in range(n_maps)]
    if diff:
        o = _diff_finish(a[0], a[1], a[2], a[3], lam_ref[0], subg_ref[...], post_scale)
    else:
        o = jnp.where(_lane(a[0].shape) < HEAD_DIM, a[0], a[1])
    o_ref[0] = o.astype(BF16)


def _dense(qkv, cache_k, cache_v, layer, lam, subg, *, diff, post_scale, tq=256, tk=512):
    b, n, _ = qkv.shape
    n_maps = 4 if diff else 2
    if diff:
        qb, kb, vb, cb = 14, 16, 18, 4
        kmap = lambda bi, j, i: (bi, 0, kb + j)
        vmap = lambda bi, j, i: (bi, 0, vb + j)
        cmap = lambda bi, j, i: (bi, layer, 0, cb + j)
    else:
        qb, kb, vb, cb = 10, 12, 13, 3
        kmap = lambda bi, j, i: (bi, 0, kb)
        vmap = lambda bi, j, i: (bi, 0, vb)
        cmap = lambda bi, j, i: (bi, layer, 0, cb)
    m = cache_k.shape[2]
    return pl.pallas_call(
        functools.partial(_dense_kernel, n_maps=n_maps, tk=tk, diff=diff, post_scale=post_scale),
        grid=(b, 2, n // tq),
        in_specs=[pl.BlockSpec(memory_space=pltpu.SMEM),
                  pl.BlockSpec((1, tq, LANES), lambda bi, j, i: (bi, i, qb + j)),
                  pl.BlockSpec((1, n, LANES), kmap),
                  pl.BlockSpec((1, n, LANES), vmap),
                  pl.BlockSpec((1, 1, m, LANES), cmap),
                  pl.BlockSpec((1, 1, m, LANES), cmap),
                  pl.BlockSpec((1, LANES), lambda bi, j, i: (0, 0))],
        out_specs=pl.BlockSpec((1, tq, LANES), lambda bi, j, i: (bi, i, j)),
        out_shape=jax.ShapeDtypeStruct((b, n, 2 * LANES), BF16),
        scratch_shapes=[pltpu.VMEM((n_maps, tq, 1), F32), pltpu.VMEM((n_maps, tq, 1), F32),
                        pltpu.VMEM((n_maps, tq, LANES), F32)],
        compiler_params=pltpu.CompilerParams(vmem_limit_bytes=VMEM_LIMIT),
        name="diff_attn" if diff else "dense_attn",
    )(lam, qkv, qkv, qkv, cache_k, cache_v, subg)


_ROUTE_E1, _ROUTE_E2, _ROUTE_W1, _ROUTE_W2 = 0, 1, 2, 3


def _oproj_kernel(*refs, widths):
    nm = len(widths)
    mix_refs = refs[:nm]
    x_ref, g1_ref, sc_ref, sh_ref, g_ref, wo_ref, wr_ref, xo_ref, h_ref, route_ref = refs[nm:]
    acc = None
    off = 0
    for mref, wd in zip(mix_refs, widths):
        part = jnp.dot(mref[0], wo_ref[off:off + wd, :], preferred_element_type=F32)
        acc = part if acc is None else acc + part
        off += wd
    x = x_ref[0] + g1_ref[0] * acc
    xo_ref[0] = x
    ms = jnp.mean(x * x, axis=-1, keepdims=True)
    h = (x * lax.rsqrt(ms + EPS) * g_ref[...]) * (1.0 + sc_ref[0]) + sh_ref[0]
    h_ref[0] = h
    logits = jnp.dot(h, wr_ref[...], preferred_element_type=F32, precision=lax.Precision.HIGHEST)
    lane = _lane(logits.shape).astype(F32)
    big = float(LANES)
    glog = jnp.where(lane < N_GROUPS, logits, -jnp.inf)
    gmax = glog.max(axis=-1, keepdims=True)
    gsel = jnp.where(glog == gmax, lane, big).min(axis=-1, keepdims=True)
    pg = 1.0 / jnp.exp(glog - gmax).sum(axis=-1, keepdims=True)
    e_lo = N_GROUPS + EXPERTS_PER_GROUP * gsel
    el = jnp.where((lane >= e_lo) & (lane < e_lo + EXPERTS_PER_GROUP), logits, -jnp.inf)
    v1 = el.max(axis=-1, keepdims=True)
    i1 = jnp.where(el == v1, lane, big).min(axis=-1, keepdims=True)
    el2 = jnp.where(lane == i1, -jnp.inf, el)
    v2 = el2.max(axis=-1, keepdims=True)
    i2 = jnp.where(el2 == v2, lane, big).min(axis=-1, keepdims=True)
    t = jnp.exp(v2 - v1)
    w1 = pg / (1.0 + t)
    w2 = pg * t / (1.0 + t)
    rec = jnp.where(lane == _ROUTE_E1, i1 - N_GROUPS,
                    jnp.where(lane == _ROUTE_E2, i2 - N_GROUPS,
                              jnp.where(lane == _ROUTE_W1, w1, jnp.where(lane == _ROUTE_W2, w2, 0.0))))
    route_ref[0] = rec


def _oproj_router(mixes, x, g1, sc2, sh2, g, wo, wr, tn=256):
    b, n, _ = x.shape
    widths = tuple(m.shape[-1] for m in mixes)
    row = lambda bi, i: (bi, i, 0)
    per_b = lambda bi, i: (bi, 0, 0)
    const = lambda bi, i: (0, 0)
    in_specs = ([pl.BlockSpec((1, tn, wd), row) for wd in widths]
                + [pl.BlockSpec((1, tn, D_MODEL), row)]
                + [pl.BlockSpec((1, 1, D_MODEL), per_b)] * 3
                + [pl.BlockSpec((1, D_MODEL), const),
                   pl.BlockSpec((MIX_WIDTH, D_MODEL), const),
                   pl.BlockSpec((D_MODEL, LANES), const)])
    return pl.pallas_call(
        functools.partial(_oproj_kernel, widths=widths),
        grid=(b, n // tn),
        in_specs=in_specs,
        out_specs=[pl.BlockSpec((1, tn, D_MODEL), row), pl.BlockSpec((1, tn, D_MODEL), row),
                   pl.BlockSpec((1, tn, LANES), row)],
        out_shape=[jax.ShapeDtypeStruct((b, n, D_MODEL), F32), jax.ShapeDtypeStruct((b, n, D_MODEL), F32),
                   jax.ShapeDtypeStruct((b, n, LANES), F32)],
        compiler_params=pltpu.CompilerParams(vmem_limit_bytes=VMEM_LIMIT),
        name="oproj_router",
    )(*mixes, x, g1, sc2, sh2, g, wo, wr)


def _dispatch_plan(route):
    t = route.shape[0]
    e = route[:, :2].astype(jnp.int32).reshape(-1)
    onehot = (e[:, None] == jnp.arange(N_EXPERTS)[None, :]).astype(jnp.int32)
    rank = jnp.sum((jnp.cumsum(onehot, axis=0) - 1) * onehot, axis=1)
    counts = jnp.sum(onehot, axis=0)
    padded = ((counts + MOE_TILE - 1) // MOE_TILE) * MOE_TILE
    ends = jnp.cumsum(padded)
    starts = ends - padded
    slot = starts[e] + rank
    n_tiles = (2 * t) // MOE_TILE + N_EXPERTS
    tile_id = jnp.arange(n_tiles, dtype=jnp.int32)
    used = ends[-1] // MOE_TILE
    te = jnp.searchsorted(ends // MOE_TILE, jnp.minimum(tile_id, used - 1), side="right").astype(jnp.int32)
    valid = (tile_id < used).astype(jnp.int32)
    slot_token = jnp.zeros((n_tiles * MOE_TILE,), jnp.int32).at[slot].set(jnp.arange(2 * t, dtype=jnp.int32) // 2)
    return te, valid, slot_token.reshape(n_tiles, 1, MOE_TILE), slot.reshape(t, 2).astype(jnp.int32)


def _expert_kernel(te_ref, valid_ref, idx_ref, h_hbm, wg_ref, wu_ref, wd_ref, o_ref, hbuf, wgb, wub, wdb, sem):
    i = pl.program_id(0)

    @pl.when(valid_ref[i] == 1)
    def _():
        def issue(r, carry):
            tok = idx_ref[0, 0, r]
            pltpu.make_async_copy(h_hbm.at[pl.ds(tok, 1), :], hbuf.at[pl.ds(r, 1), :], sem).start()
            return carry

        lax.fori_loop(0, MOE_TILE, issue, 0)

        @pl.when((i == 0) | (te_ref[i] != te_ref[jnp.maximum(i - 1, 0)]))
        def _():
            wgb[...] = wg_ref[0, 0].astype(BF16)
            wub[...] = wu_ref[0, 0].astype(BF16)
            wdb[...] = wd_ref[0, 0].astype(BF16)

        pltpu.make_async_copy(h_hbm.at[pl.ds(0, MOE_TILE), :], hbuf, sem).wait()
        hb = hbuf[...].astype(BF16)
        gate = jnp.dot(hb, wgb[...], preferred_element_type=F32)
        up = jnp.dot(hb, wub[...], preferred_element_type=F32)
        hdn = gate * (1.0 / (1.0 + jnp.exp(-gate))) * up
        o_ref[...] = jnp.dot(hdn.astype(BF16), wdb[...], preferred_element_type=F32)

    @pl.when(valid_ref[i] == 0)
    def _():
        o_ref[...] = jnp.zeros_like(o_ref)


def _expert_tiles(te, valid, slot_token, h, w_gate, w_up, w_down, layer):
    n_tiles = slot_token.shape[0]
    grid_spec = pltpu.PrefetchScalarGridSpec(
        num_scalar_prefetch=2,
        grid=(n_tiles,),
        in_specs=[pl.BlockSpec((1, 1, MOE_TILE), lambda i, te, va: (i, 0, 0), memory_space=pltpu.SMEM),
                  pl.BlockSpec(memory_space=pl.ANY),
                  pl.BlockSpec((1, 1, D_MODEL, EXPERT_FF), lambda i, te, va: (layer, te[i], 0, 0)),
                  pl.BlockSpec((1, 1, D_MODEL, EXPERT_FF), lambda i, te, va: (layer, te[i], 0, 0)),
                  pl.BlockSpec((1, 1, EXPERT_FF, D_MODEL), lambda i, te, va: (layer, te[i], 0, 0))],
        out_specs=pl.BlockSpec((MOE_TILE, D_MODEL), lambda i, te, va: (i, 0)),
        scratch_shapes=[pltpu.VMEM((MOE_TILE, D_MODEL), F32),
                        pltpu.VMEM((D_MODEL, EXPERT_FF), BF16), pltpu.VMEM((D_MODEL, EXPERT_FF), BF16),
                        pltpu.VMEM((EXPERT_FF, D_MODEL), BF16), pltpu.SemaphoreType.DMA(())])
    return pl.pallas_call(
        _expert_kernel,
        grid_spec=grid_spec,
        out_shape=jax.ShapeDtypeStruct((n_tiles * MOE_TILE, D_MODEL), F32),
        compiler_params=pltpu.CompilerParams(vmem_limit_bytes=VMEM_LIMIT, dimension_semantics=("arbitrary",)),
        name="moe_experts",
    )(te, valid, slot_token, h, w_gate, w_up, w_down)


def _combine_kernel(slot_ref, x_ref, g2_ref, route_ref, fg_ref, o_hbm, y_ref, obuf, sem, *, tn, final):
    def issue(r, carry):
        for k in range(2):
            s = slot_ref[0, 0, 2 * r + k]
            pltpu.make_async_copy(o_hbm.at[pl.ds(s, 1), :], obuf.at[k, pl.ds(r, 1), :], sem).start()
        return carry

    lax.fori_loop(0, tn, issue, 0)
    for k in range(2):
        pltpu.make_async_copy(o_hbm.at[pl.ds(0, tn), :], obuf.at[k], sem).wait()
    rec = route_ref[0]
    w1 = rec[:, _ROUTE_W1:_ROUTE_W1 + 1]
    w2 = rec[:, _ROUTE_W2:_ROUTE_W2 + 1]
    y = x_ref[0] + g2_ref[0] * (w1 * obuf[0] + w2 * obuf[1])
    if final:
        ms = jnp.mean(y * y, axis=-1, keepdims=True)
        y = y * lax.rsqrt(ms + EPS) * fg_ref[...]
    y_ref[0] = y


def _combine(slots, x, g2, route, final_g, o, *, final, tn=256):
    b, n, _ = x.shape
    nt = n // tn
    slots3 = slots.reshape(b * nt, 1, 2 * tn)
    row = lambda bi, i: (bi, i, 0)
    return pl.pallas_call(
        functools.partial(_combine_kernel, tn=tn, final=final),
        grid=(b, nt),
        in_specs=[pl.BlockSpec((1, 1, 2 * tn), lambda bi, i: (bi * nt + i, 0, 0), memory_space=pltpu.SMEM),
                  pl.BlockSpec((1, tn, D_MODEL), row),
                  pl.BlockSpec((1, 1, D_MODEL), lambda bi, i: (bi, 0, 0)),
                  pl.BlockSpec((1, tn, LANES), row),
                  pl.BlockSpec((1, D_MODEL), lambda bi, i: (0, 0)),
                  pl.BlockSpec(memory_space=pl.ANY)],
        out_specs=pl.BlockSpec((1, tn, D_MODEL), row),
        out_shape=jax.ShapeDtypeStruct((b, n, D_MODEL), F32),
        scratch_shapes=[pltpu.VMEM((2, tn, D_MODEL), F32), pltpu.SemaphoreType.DMA(())],
        compiler_params=pltpu.CompilerParams(vmem_limit_bytes=VMEM_LIMIT),
        name="moe_combine",
    )(slots3, x, g2, route, final_g, o)


def _moe(x, h, route, g2, final_g, w_gate, w_up, w_down, layer, *, final):
    b, n, _ = x.shape
    te, valid, slot_token, slots = _dispatch_plan(route.reshape(b * n, LANES))
    o = _expert_tiles(te, valid, slot_token, h.reshape(b * n, D_MODEL), w_gate, w_up, w_down, layer)
    return _combine(slots, x, g2, route, final_g, o, final=final)


def _layer(x, mod, layer, prm, *, latent, cache=None, rope_tabs=None, final):
    sh1, sc1, g1, sh2, sc2, g2 = mod
    qkv = _qkv_proj(x, sc1, sh1, prm["norm1_g"], prm["w_in"], prm["gains"], rope_tabs if latent else None)
    if latent:
        cache_k, cache_v = cache
        mixes = [_natten(qkv, cache_k, cache_v, layer, prm["na_bias"]),
                 _window(qkv, cache_k, cache_v, layer, prm["sink"]),
                 _dense(qkv, cache_k, cache_v, layer, prm["lam"], prm["subg"], diff=False, post_scale=1.0),
                 _dense(qkv, cache_k, cache_v, layer, prm["lam"], prm["subg"], diff=True,
                        post_scale=prm["post_scale"])]
    else:
        mixes = [_ctx_attention(qkv, prm["sink"], prm["lam"], prm["subg"], prm["post_scale"])]
    x1, h2, route = _oproj_router(mixes, x, g1, sc2, sh2, prm["norm2_g"], prm["w_out"], prm["w_router"])
    y = _moe(x1, h2, route, g2, prm["final_g"], prm["w_gate"], prm["w_up"], prm["w_down"], layer, final=final)
    return y, qkv


def kernel(x_prompt, x_sample, cache_k, cache_v, c, c_ctx, w_ada, b_ada, norm1_g, norm2_g, w_in, w_out, na_rpb,
           sink_logit, qn_g, kn_g, lam_q1, lam_k1, lam_q2, lam_k2, subln_g, w_router_group, w_router_expert,
           w_gate, w_up, w_down, final_g):
    nb, n_ctx, _ = x_prompt.shape
    db, n_lat, _ = x_sample.shape
    past = cache_k.shape[2]
    cond = jnp.concatenate([c_ctx[None, :], c, jnp.zeros((16 - 1 - db, D_MODEL), F32)], axis=0)
    mod = _modulation(cond, w_ada, b_ada)
    ck = cache_k.reshape(db, DEPTH, past, CACHE_WIDTH)
    cv = cache_v.reshape(db, DEPTH, past, CACHE_WIDTH)
    rope_tabs = _rope_tables(n_lat)
    qkv_perm = _qkv_perm()
    mix_perm = _mix_perm()
    xp, xs = x_prompt, x_sample
    new_k, new_v = [], []
    for l in range(DEPTH):
        lam_init = 0.8 - 0.6 * math.exp(-0.3 * l)
        lam = (jnp.exp(jnp.sum(lam_q1[l] * lam_k1[l])) - jnp.exp(jnp.sum(lam_q2[l] * lam_k2[l])) + lam_init)
        prm = {
            "norm1_g": norm1_g[l][None, :], "norm2_g": norm2_g[l][None, :], "final_g": final_g[None, :],
            "w_in": w_in[l][:, qkv_perm].astype(BF16),
            "w_out": w_out[l][mix_perm, :].astype(BF16),
            "gains": jnp.stack([jnp.tile(qn_g[l], 2), jnp.tile(kn_g[l], 2)]),
            "na_bias": _natten_bias(na_rpb[l], n_lat // GRID_W),
            "sink": sink_logit[l],
            "lam": lam.reshape(1).astype(F32),
            "subg": jnp.tile(subln_g[l], 2)[None, :],
            "post_scale": 1.0 - lam_init,
            "w_router": jnp.concatenate(
                [w_router_group[l], w_router_expert[l],
                 jnp.zeros((D_MODEL, LANES - N_GROUPS - N_EXPERTS), F32)], axis=1),
            "w_gate": w_gate, "w_up": w_up, "w_down": w_down,
        }
        m = mod[l].reshape(16, 6, 1, D_MODEL)
        mod_ctx = [jnp.broadcast_to(m[0:1, s], (nb, 1, D_MODEL)) for s in range(6)]
        mod_lat = [m[1:1 + db, s] for s in range(6)]
        final = l == DEPTH - 1
        xp, qkv_c = _layer(xp, mod_ctx, l, prm, latent=False, final=final)
        xs, _ = _layer(xs, mod_lat, l, prm, latent=True, cache=(ck, cv), rope_tabs=rope_tabs, final=final)
        new_k.append(jnp.concatenate([qkv_c[..., 256:512], qkv_c[..., 1024:1152], qkv_c[..., 1536:1664],
                                      qkv_c[..., 2048:2304]], axis=-1))
        new_v.append(jnp.concatenate([qkv_c[..., 512:768], qkv_c[..., 1152:1280], qkv_c[..., 1664:1792],
                                      qkv_c[..., 2304:2560]], axis=-1))
    heads = CACHE_WIDTH // HEAD_DIM
    new_cache_k = jnp.stack(new_k, axis=1).astype(F32).reshape(nb, DEPTH, n_ctx, heads, HEAD_DIM)
    new_cache_v = jnp.stack(new_v, axis=1).astype(F32).reshape(nb, DEPTH, n_ctx, heads, HEAD_DIM)
    return (xp, xs, new_cache_k, new_cache_v)
```

```python
import functools
import math

import numpy as np
import jax
import jax.numpy as jnp
from jax import lax
from jax.experimental import pallas as pl
from jax.experimental.pallas import tpu as pltpu

D_MODEL = 1024
DEPTH = 2
GRID_W = 64
HEAD_DIM = 64
D_HALF = HEAD_DIM // 2
NA_ROWS = 8
NA_COLS = 16
WINDOW = 128
ROPE_BASE = 10000.0
N_GROUPS = 4
EXPERTS_PER_GROUP = 8
N_EXPERTS = N_GROUPS * EXPERTS_PER_GROUP
EXPERT_FF = 512
EPS = 1e-6
NEG_INF = -1e30
QKV_WIDTH = 2560
MIX_WIDTH = 1024
CACHE_WIDTH = 768

LANES = 128
MOE_TILE = 256
VMEM_LIMIT = 48 * 1024 * 1024

F32 = jnp.float32
BF16 = jnp.bfloat16

_GQA_ORDER = (0, 2, 1, 3)


def _permute_gqa_heads(w, bases, axis):
    pieces = []
    pos = 0
    for base in bases:
        pieces.append(lax.slice_in_dim(w, pos, base, axis=axis))
        for h in _GQA_ORDER:
            pieces.append(lax.slice_in_dim(w, base + h * HEAD_DIM, base + (h + 1) * HEAD_DIM, axis=axis))
        pos = base + 4 * HEAD_DIM
    pieces.append(lax.slice_in_dim(w, pos, w.shape[axis], axis=axis))
    return jnp.concatenate(pieces, axis=axis)


def _permute_w_in(w):
    return _permute_gqa_heads(w, (768, 1280), 1)


def _permute_w_out(w):
    return _permute_gqa_heads(w, (256, 512), 0)


def _lane(shape):
    return lax.broadcasted_iota(jnp.int32, shape, len(shape) - 1)


def _head_rms(x):
    lo = _lane(x.shape) < HEAD_DIM
    sq = x * x
    s_lo = jnp.sum(jnp.where(lo, sq, 0.0), axis=-1, keepdims=True)
    s_hi = jnp.sum(jnp.where(lo, 0.0, sq), axis=-1, keepdims=True)
    return jnp.where(lo, lax.rsqrt(s_lo * (1.0 / HEAD_DIM) + EPS), lax.rsqrt(s_hi * (1.0 / HEAD_DIM) + EPS))


def _swap_pairs(x):
    n = x.shape[-1]
    even = (_lane(x.shape) % 2) == 0
    return jnp.where(even, pltpu.roll(x, n - 1, axis=x.ndim - 1), pltpu.roll(x, 1, axis=x.ndim - 1))


def _qk(qm, k):
    return lax.dot_general(qm, k, (((1,), (1,)), ((), ())), preferred_element_type=F32)


def _mask_q(q, lo, hi):
    lane = _lane(q.shape)
    return jnp.where((lane >= lo) & (lane < hi), q, jnp.zeros_like(q))


def _joint_softmax_pv(scores, values, extra_logit=None):
    m = scores[0].max(axis=-1, keepdims=True)
    for s in scores[1:]:
        m = jnp.maximum(m, s.max(axis=-1, keepdims=True))
    if extra_logit is not None:
        m = jnp.maximum(m, extra_logit)
    l = None
    o = None
    for s, v in zip(scores, values):
        p = jnp.exp(s - m)
        ls = p.sum(axis=-1, keepdims=True)
        os_ = jnp.dot(p.astype(BF16), v, preferred_element_type=F32)
        l = ls if l is None else l + ls
        o = os_ if o is None else o + os_
    if extra_logit is not None:
        l = l + jnp.exp(extra_logit - m)
    return o / l


def _diff_finish(a1_h0, a2_h0, a1_h1, a2_h1, lam, subg, post_scale):
    lo = _lane(a1_h0.shape) < HEAD_DIM
    d = jnp.where(lo, a1_h0 - lam * a2_h0, a1_h1 - lam * a2_h1)
    return d * _head_rms(d) * subg * post_scale


def _mod_kernel(c_ref, w_ref, b_ref, o_ref):
    c = c_ref[...]
    s = c * (1.0 / (1.0 + jnp.exp(-c)))
    o_ref[0] = jnp.dot(s.astype(BF16), w_ref[0].astype(BF16), preferred_element_type=F32) + b_ref[0]


def _modulation(cond, w_ada, b_ada):
    r = cond.shape[0]
    tn = 1536
    return pl.pallas_call(
        _mod_kernel,
        grid=(DEPTH, 6 * D_MODEL // tn),
        in_specs=[pl.BlockSpec((r, D_MODEL), lambda l, j: (0, 0)),
                  pl.BlockSpec((1, D_MODEL, tn), lambda l, j: (l, 0, j)),
                  pl.BlockSpec((1, 1, tn), lambda l, j: (l, 0, j))],
        out_specs=pl.BlockSpec((1, r, tn), lambda l, j: (l, 0, j)),
        out_shape=jax.ShapeDtypeStruct((DEPTH, r, 6 * D_MODEL), F32),
        compiler_params=pltpu.CompilerParams(vmem_limit_bytes=VMEM_LIMIT),
        name="modulation",
    )(cond, w_ada, b_ada.reshape(DEPTH, 1, 6 * D_MODEL))


_SCALE64 = HEAD_DIM ** -0.5
_SCALE32 = D_HALF ** -0.5
_BLOCK_KIND = {0: (None, None, _SCALE64), 1: (None, None, _SCALE64),
               6: (None, 64, _SCALE64), 7: (None, 64, _SCALE64), 8: (None, 64, 1.0),
               10: (0, 64, _SCALE64), 11: (0, 64, _SCALE64), 12: (1, 64, 1.0),
               14: (None, 32, _SCALE32), 15: (None, 32, _SCALE32), 16: (None, 32, 1.0), 17: (None, 32, 1.0)}


def _qkv_kernel(x_ref, sc_ref, sh_ref, g_ref, w_ref, gains_ref, *rest, rope):
    if rope:
        cos64_ref, sin64_ref, cos32_ref, sin32_ref, o_ref = rest
    else:
        (o_ref,) = rest
    x = x_ref[0]
    ms = jnp.mean(x * x, axis=-1, keepdims=True)
    h = (x * lax.rsqrt(ms + EPS) * g_ref[...]) * (1.0 + sc_ref[0]) + sh_ref[0]
    hb = h.astype(BF16)
    for c in range(QKV_WIDTH // 256):
        acc = jnp.dot(hb, w_ref[:, c * 256:(c + 1) * 256], preferred_element_type=F32)
        for half in range(2):
            j = 2 * c + half
            y = acc[:, half * LANES:(half + 1) * LANES]
            gain_row, rope_kind, scale = _BLOCK_KIND.get(j, (None, None, 1.0))
            if gain_row is not None:
                y = y * _head_rms(y) * gains_ref[gain_row:gain_row + 1, :]
            if rope and rope_kind is not None:
                cos_ref, sin_ref = (cos64_ref, sin64_ref) if rope_kind == 64 else (cos32_ref, sin32_ref)
                y = y * cos_ref[...] + _swap_pairs(y) * sin_ref[...]
            if scale != 1.0:
                y = y * scale
            o_ref[0, :, j * LANES:(j + 1) * LANES] = y.astype(BF16)


def _qkv_proj(x, sc, sh, g, w, gains, rope_tabs, tn=256):
    b, n, _ = x.shape
    rope = rope_tabs is not None
    in_specs = [pl.BlockSpec((1, tn, D_MODEL), lambda bi, i: (bi, i, 0)),
                pl.BlockSpec((1, 1, D_MODEL), lambda bi, i: (bi, 0, 0)),
                pl.BlockSpec((1, 1, D_MODEL), lambda bi, i: (bi, 0, 0)),
                pl.BlockSpec((1, D_MODEL), lambda bi, i: (0, 0)),
                pl.BlockSpec((D_MODEL, QKV_WIDTH), lambda bi, i: (0, 0)),
                pl.BlockSpec((2, LANES), lambda bi, i: (0, 0))]
    args = [x, sc, sh, g, w, gains]
    if rope:
        in_specs += [pl.BlockSpec((tn, LANES), lambda bi, i: (i, 0))] * 4
        args += list(rope_tabs)
    return pl.pallas_call(
        functools.partial(_qkv_kernel, rope=rope),
        grid=(b, n // tn),
        in_specs=in_specs,
        out_specs=pl.BlockSpec((1, tn, QKV_WIDTH), lambda bi, i: (bi, i, 0)),
        out_shape=jax.ShapeDtypeStruct((b, n, QKV_WIDTH), BF16),
        compiler_params=pltpu.CompilerParams(vmem_limit_bytes=VMEM_LIMIT),
        name="qkv_rope" if rope else "qkv_ctx",
    )(*args)


def _rope_tables(n):
    t = jnp.arange(n, dtype=jnp.int32)
    row = (t // GRID_W).astype(F32)
    col = (t % GRID_W).astype(F32)
    out = []
    for rot_dim in (HEAD_DIM, D_HALF):
        nf = rot_dim // 4
        inv = ROPE_BASE ** (-jnp.arange(nf, dtype=F32) / nf)
        ang = jnp.concatenate([row[:, None] * inv, col[:, None] * inv], axis=-1)
        cos = jnp.repeat(jnp.cos(ang), 2, axis=-1)
        sin = jnp.repeat(jnp.sin(ang), 2, axis=-1)
        sign = jnp.asarray(np.tile(np.array([-1.0, 1.0], np.float32), rot_dim // 2))
        reps = LANES // rot_dim
        out += [jnp.tile(cos, (1, reps)), jnp.tile(sin * sign, (1, reps))]
    return tuple(out)


def _ctx_attn_kernel(sink_ref, lam_ref, qkv_ref, subg_ref, o_ref, *, post_scale):
    def blk(j):
        return qkv_ref[0, :, j * LANES:(j + 1) * LANES]

    def two_head(qj, kj, vj, sinks):
        q, k, v = blk(qj), blk(kj), blk(vj)
        outs = []
        for mi in range(2):
            s = _qk(_mask_q(q, mi * HEAD_DIM, (mi + 1) * HEAD_DIM), k)
            outs.append(_joint_softmax_pv([s], [v], None if sinks is None else sinks[mi]))
        return jnp.where(_lane(outs[0].shape) < HEAD_DIM, outs[0], outs[1])

    for b2 in range(2):
        o_ref[0, :, b2 * LANES:(b2 + 1) * LANES] = two_head(b2, 2 + b2, 4 + b2, None).astype(BF16)
        o_ref[0, :, (2 + b2) * LANES:(3 + b2) * LANES] = two_head(
            6 + b2, 8, 9, (sink_ref[b2], sink_ref[b2 + 2])).astype(BF16)
        o_ref[0, :, (4 + b2) * LANES:(5 + b2) * LANES] = two_head(10 + b2, 12, 13, None).astype(BF16)
        q, k, v = blk(14 + b2), blk(16 + b2), blk(18 + b2)
        a = [_joint_softmax_pv([_qk(_mask_q(q, mi * D_HALF, (mi + 1) * D_HALF), k)], [v]) for mi in range(4)]
        o_ref[0, :, (6 + b2) * LANES:(7 + b2) * LANES] = _diff_finish(
            a[0], a[1], a[2], a[3], lam_ref[0], subg_ref[...], post_scale).astype(BF16)


def _ctx_attention(qkv, sink, lam, subg, post_scale):
    b, n, _ = qkv.shape
    return pl.pallas_call(
        functools.partial(_ctx_attn_kernel, post_scale=post_scale),
        grid=(b,),
        in_specs=[pl.BlockSpec(memory_space=pltpu.SMEM),
                  pl.BlockSpec(memory_space=pltpu.SMEM),
                  pl.BlockSpec((1, n, QKV_WIDTH), lambda bi: (bi, 0, 0)),
                  pl.BlockSpec((1, LANES), lambda bi: (0, 0))],
        out_specs=pl.BlockSpec((1, n, MIX_WIDTH), lambda bi: (bi, 0, 0)),
        out_shape=jax.ShapeDtypeStruct((b, n, MIX_WIDTH), BF16),
        compiler_params=pltpu.CompilerParams(vmem_limit_bytes=VMEM_LIMIT),
        name="ctx_attention",
    )(sink, lam, qkv, subg)


_NA_QROWS = 2
_NA_KBLOCKS = 5
_NA_CASES = (0, 2, 4, 60, 62)


def _natten_bias(rpb, rows):
    assert rows == 64
    nblk = rows // _NA_QROWS
    r0s = np.array(_NA_CASES)
    nq, nk = _NA_QROWS * GRID_W, _NA_KBLOCKS * LANES
    qr = r0s[:, None] + (np.arange(nq) // GRID_W)[None, :]
    qc = np.arange(nq) % GRID_W
    start = np.clip((r0s - NA_ROWS // 2) // 2, 0, nblk - _NA_KBLOCKS)
    kr = 2 * start[:, None] + (np.arange(nk) // GRID_W)[None, :]
    kc = np.arange(nk) % GRID_W
    rs = np.clip(qr - NA_ROWS // 2, 0, rows - NA_ROWS)
    vrow = (kr[:, None, :] >= rs[:, :, None]) & (kr[:, None, :] < rs[:, :, None] + NA_ROWS)
    cs = np.clip(qc - NA_COLS // 2, 0, GRID_W - NA_COLS)
    vcol = (kc[None, :] >= cs[:, None]) & (kc[None, :] < cs[:, None] + NA_COLS)
    valid = vrow & vcol[None]
    w = GRID_W
    nh, na, nc = rpb.shape
    front = (w - 1) - (NA_COLS - 1)
    line = jnp.pad(rpb.astype(F32), ((0, 0), (0, 0), (front, 2 * w - front - nc)))
    toep = jnp.tile(line, (1, 1, w))[:, :, :w * (2 * w - 1)].reshape(nh, na, w, 2 * w - 1)[..., w - 1:]
    qr_b = r0s[:, None] + np.arange(_NA_QROWS)[None, :]
    kr_b = 2 * start[:, None] + np.arange(nk // w)[None, :]
    a_idx = np.clip(kr_b[:, None, :] - qr_b[:, :, None] + (NA_ROWS - 1), 0, na - 1)
    bias = toep[:, a_idx]
    bias = bias.transpose(0, 1, 2, 4, 3, 5).reshape(nh, len(_NA_CASES), nq, nk)
    return jnp.where(valid[None], bias, NEG_INF)


def _natten_kernel(q_ref, k0, k1, k2, k3, k4, v0, v1, v2, v3, v4, kc_ref, vc_ref, bias_ref, o_ref):
    q = q_ref[0]
    k = jnp.concatenate([r[0] for r in (k0, k1, k2, k3, k4)], axis=0)
    v = jnp.concatenate([r[0] for r in (v0, v1, v2, v3, v4)], axis=0)
    kc = kc_ref[0, 0].astype(BF16)
    vc = vc_ref[0, 0].astype(BF16)
    outs = []
    for mi in range(2):
        qm = _mask_q(q, mi * HEAD_DIM, (mi + 1) * HEAD_DIM)
        s_loc = _qk(qm, k) + bias_ref[mi, 0]
        s_ctx = _qk(qm, kc)
        outs.append(_joint_softmax_pv([s_ctx, s_loc], [vc, v]))
    o_ref[0] = jnp.where(_lane(outs[0].shape) < HEAD_DIM, outs[0], outs[1]).astype(BF16)


def _natten(qkv, cache_k, cache_v, layer, bias):
    b, n, _ = qkv.shape
    nblk = n // LANES

    def start(i):
        return jnp.clip(i - 2, 0, nblk - _NA_KBLOCKS)

    def case(i):
        return jnp.where(i == 0, 0, jnp.where(i == 1, 1, jnp.where(i == nblk - 2, 3, jnp.where(i == nblk - 1, 4, 2))))

    def kv_spec(colblk, jj):
        return pl.BlockSpec((1, LANES, LANES), lambda bi, j, i: (bi, start(i) + jj, colblk + j))

    in_specs = ([pl.BlockSpec((1, LANES, LANES), lambda bi, j, i: (bi, i, j))]
                + [kv_spec(2, jj) for jj in range(_NA_KBLOCKS)]
                + [kv_spec(4, jj) for jj in range(_NA_KBLOCKS)]
                + [pl.BlockSpec((1, 1, cache_k.shape[2], LANES), lambda bi, j, i: (bi, layer, 0, j))] * 2
                + [pl.BlockSpec((2, 1, LANES, _NA_KBLOCKS * LANES), lambda bi, j, i: (j, case(i), 0, 0))])
    return pl.pallas_call(
        _natten_kernel,
        grid=(b, 2, nblk),
        in_specs=in_specs,
        out_specs=pl.BlockSpec((1, LANES, LANES), lambda bi, j, i: (bi, i, j)),
        out_shape=jax.ShapeDtypeStruct((b, n, 2 * LANES), BF16),
        compiler_params=pltpu.CompilerParams(vmem_limit_bytes=VMEM_LIMIT),
        name="natten",
    )(qkv, *([qkv] * (2 * _NA_KBLOCKS)), cache_k, cache_v, bias)


def _window_kernel(sink_ref, q_ref, k0, k1, k2, v0, v1, v2, kc_ref, vc_ref, o_ref, *, n):
    j = pl.program_id(1)
    i = pl.program_id(2)
    q = q_ref[0]
    k = jnp.concatenate([r[0] for r in (k0, k1, k2)], axis=0)
    v = jnp.concatenate([r[0] for r in (v0, v1, v2)], axis=0)
    kc = kc_ref[0, 0].astype(BF16)
    vc = vc_ref[0, 0].astype(BF16)
    r = lax.broadcasted_iota(jnp.int32, (LANES, 3 * LANES), 0)
    c = lax.broadcasted_iota(jnp.int32, (LANES, 3 * LANES), 1)
    kp = (i - 1) * LANES + c
    ok = (kp >= 0) & (kp < n) & (jnp.abs(c - LANES - r) <= WINDOW)
    outs = []
    for mi in range(2):
        qm = _mask_q(q, mi * HEAD_DIM, (mi + 1) * HEAD_DIM)
        s_loc = jnp.where(ok, _qk(qm, k), NEG_INF)
        s_ctx = _qk(qm, kc)
        outs.append(_joint_softmax_pv([s_ctx, s_loc], [vc, v], sink_ref[j + 2 * mi]))
    o_ref[0] = jnp.where(_lane(outs[0].shape) < HEAD_DIM, outs[0], outs[1]).astype(BF16)


def _window(qkv, cache_k, cache_v, layer, sink):
    b, n, _ = qkv.shape
    nblk = n // LANES

    def kv_spec(colblk, jj):
        return pl.BlockSpec((1, LANES, LANES), lambda bi, j, i: (bi, jnp.clip(i - 1 + jj, 0, nblk - 1), colblk))

    in_specs = ([pl.BlockSpec(memory_space=pltpu.SMEM),
                 pl.BlockSpec((1, LANES, LANES), lambda bi, j, i: (bi, i, 6 + j))]
                + [kv_spec(8, jj) for jj in range(3)]
                + [kv_spec(9, jj) for jj in range(3)]
                + [pl.BlockSpec((1, 1, cache_k.shape[2], LANES), lambda bi, j, i: (bi, layer, 0, 2))] * 2)
    return pl.pallas_call(
        functools.partial(_window_kernel, n=n),
        grid=(b, 2, nblk),
        in_specs=in_specs,
        out_specs=pl.BlockSpec((1, LANES, LANES), lambda bi, j, i: (bi, i, j)),
        out_shape=jax.ShapeDtypeStruct((b, n, 2 * LANES), BF16),
        compiler_params=pltpu.CompilerParams(vmem_limit_bytes=VMEM_LIMIT),
        name="window",
    )(sink, qkv, *([qkv] * 6), cache_k, cache_v)


def _dense_kernel(lam_ref, q_ref, k_ref, v_ref, kc_ref, vc_ref, subg_ref, o_ref, m_ref, l_ref, acc_ref,
                  *, n_maps, tk, diff, post_scale):
    q = q_ref[0]
    width = LANES // n_maps
    qms = [_mask_q(q, mi * width, (mi + 1) * width) for mi in range(n_maps)]
    kc = kc_ref[0, 0].astype(BF16)
    vc = vc_ref[0, 0].astype(BF16)
    for mi in range(n_maps):
        s = _qk(qms[mi], kc)
        m = s.max(axis=-1, keepdims=True)
        p = jnp.exp(s - m)
        m_ref[mi] = m
        l_ref[mi] = p.sum(axis=-1, keepdims=True)
        acc_ref[mi] = jnp.dot(p.astype(BF16), vc, preferred_element_type=F32)

    def body(c, carry):
        k = k_ref[0, pl.ds(pl.multiple_of(c * tk, tk), tk), :]
        v = v_ref[0, pl.ds(pl.multiple_of(c * tk, tk), tk), :]
        for mi in range(n_maps):
            s = _qk(qms[mi], k)
            m_old = m_ref[mi]
            m_new = jnp.maximum(m_old, s.max(axis=-1, keepdims=True))
            alpha = jnp.exp(m_old - m_new)
            p = jnp.exp(s - m_new)
            l_ref[mi] = alpha * l_ref[mi] + p.sum(axis=-1, keepdims=True)
            acc_ref[mi] = alpha * acc_ref[mi] + jnp.dot(p.astype(BF16), v, preferred_element_type=F32)
            m_ref[mi] = m_new
        return carry

    lax.fori_loop(0, k_ref.shape[1] // tk, body, 0)
    a = [acc_ref[mi] / l_ref[mi] for mi in range(n_maps)]
    if diff:
        o = _diff_finish(a[0], a[1], a[2], a[3], lam_ref[0], subg_ref[...], post_scale)
    else:
        o = jnp.where(_lane(a[0].shape) < HEAD_DIM, a[0], a[1])
    o_ref[0] = o.astype(BF16)


def _dense(qkv, cache_k, cache_v, layer, lam, subg, *, diff, post_scale, tq=256, tk=512):
    b, n, _ = qkv.shape
    n_maps = 4 if diff else 2
    if diff:
        qb, kb, vb, cb = 14, 16, 18, 4
        kmap = lambda bi, j, i: (bi, 0, kb + j)
        vmap = lambda bi, j, i: (bi, 0, vb + j)
        cmap = lambda bi, j, i: (bi, layer, 0, cb + j)
    else:
        qb, kb, vb, cb = 10, 12, 13, 3
        kmap = lambda bi, j, i: (bi, 0, kb)
        vmap = lambda bi, j, i: (bi, 0, vb)
        cmap = lambda bi, j, i: (bi, layer, 0, cb)
    m = cache_k.shape[2]
    return pl.pallas_call(
        functools.partial(_dense_kernel, n_maps=n_maps, tk=tk, diff=diff, post_scale=post_scale),
        grid=(b, 2, n // tq),
        in_specs=[pl.BlockSpec(memory_space=pltpu.SMEM),
                  pl.BlockSpec((1, tq, LANES), lambda bi, j, i: (bi, i, qb + j)),
                  pl.BlockSpec((1, n, LANES), kmap),
                  pl.BlockSpec((1, n, LANES), vmap),
                  pl.BlockSpec((1, 1, m, LANES), cmap),
                  pl.BlockSpec((1, 1, m, LANES), cmap),
                  pl.BlockSpec((1, LANES), lambda bi, j, i: (0, 0))],
        out_specs=pl.BlockSpec((1, tq, LANES), lambda bi, j, i: (bi, i, j)),
        out_shape=jax.ShapeDtypeStruct((b, n, 2 * LANES), BF16),
        scratch_shapes=[pltpu.VMEM((n_maps, tq, 1), F32), pltpu.VMEM((n_maps, tq, 1), F32),
                        pltpu.VMEM((n_maps, tq, LANES), F32)],
        compiler_params=pltpu.CompilerParams(vmem_limit_bytes=VMEM_LIMIT),
        name="diff_attn" if diff else "dense_attn",
    )(lam, qkv, qkv, qkv, cache_k, cache_v, subg)


_ROUTE_E1, _ROUTE_E2, _ROUTE_W1, _ROUTE_W2 = 0, 1, 2, 3


def _oproj_kernel(*refs, widths):
    nm = len(widths)
    mix_refs = refs[:nm]
    x_ref, g1_ref, sc_ref, sh_ref, g_ref, wo_ref, wr_ref, xo_ref, h_ref, route_ref = refs[nm:]
    acc = None
    off = 0
    for mref, wd in zip(mix_refs, widths):
        part = jnp.dot(mref[0], wo_ref[off:off + wd, :], preferred_element_type=F32)
        acc = part if acc is None else acc + part
        off += wd
    x = x_ref[0] + g1_ref[0] * acc
    xo_ref[0] = x
    ms = jnp.mean(x * x, axis=-1, keepdims=True)
    h = (x * lax.rsqrt(ms + EPS) * g_ref[...]) * (1.0 + sc_ref[0]) + sh_ref[0]
    h_ref[0] = h
    logits = jnp.dot(h, wr_ref[...], preferred_element_type=F32, precision=lax.Precision.HIGHEST)
    lane = _lane(logits.shape).astype(F32)
    big = float(LANES)
    glog = jnp.where(lane < N_GROUPS, logits, -jnp.inf)
    gmax = glog.max(axis=-1, keepdims=True)
    gsel = jnp.where(glog == gmax, lane, big).min(axis=-1, keepdims=True)
    pg = 1.0 / jnp.exp(glog - gmax).sum(axis=-1, keepdims=True)
    e_lo = N_GROUPS + EXPERTS_PER_GROUP * gsel
    el = jnp.where((lane >= e_lo) & (lane < e_lo + EXPERTS_PER_GROUP), logits, -jnp.inf)
    v1 = el.max(axis=-1, keepdims=True)
    i1 = jnp.where(el == v1, lane, big).min(axis=-1, keepdims=True)
    el2 = jnp.where(lane == i1, -jnp.inf, el)
    v2 = el2.max(axis=-1, keepdims=True)
    i2 = jnp.where(el2 == v2, lane, big).min(axis=-1, keepdims=True)
    t = jnp.exp(v2 - v1)
    w1 = pg / (1.0 + t)
    w2 = pg * t / (1.0 + t)
    rec = jnp.where(lane == _ROUTE_E1, i1 - N_GROUPS,
                    jnp.where(lane == _ROUTE_E2, i2 - N_GROUPS,
                              jnp.where(lane == _ROUTE_W1, w1, jnp.where(lane == _ROUTE_W2, w2, 0.0))))
    route_ref[0] = rec


def _oproj_router(mixes, x, g1, sc2, sh2, g, wo, wr, tn=256):
    b, n, _ = x.shape
    widths = tuple(m.shape[-1] for m in mixes)
    row = lambda bi, i: (bi, i, 0)
    per_b = lambda bi, i: (bi, 0, 0)
    const = lambda bi, i: (0, 0)
    in_specs = ([pl.BlockSpec((1, tn, wd), row) for wd in widths]
                + [pl.BlockSpec((1, tn, D_MODEL), row)]
                + [pl.BlockSpec((1, 1, D_MODEL), per_b)] * 3
                + [pl.BlockSpec((1, D_MODEL), const),
                   pl.BlockSpec((MIX_WIDTH, D_MODEL), const),
                   pl.BlockSpec((D_MODEL, LANES), const)])
    return pl.pallas_call(
        functools.partial(_oproj_kernel, widths=widths),
        grid=(b, n // tn),
        in_specs=in_specs,
        out_specs=[pl.BlockSpec((1, tn, D_MODEL), row), pl.BlockSpec((1, tn, D_MODEL), row),
                   pl.BlockSpec((1, tn, LANES), row)],
        out_shape=[jax.ShapeDtypeStruct((b, n, D_MODEL), F32), jax.ShapeDtypeStruct((b, n, D_MODEL), F32),
                   jax.ShapeDtypeStruct((b, n, LANES), F32)],
        compiler_params=pltpu.CompilerParams(vmem_limit_bytes=VMEM_LIMIT),
        name="oproj_router",
    )(*mixes, x, g1, sc2, sh2, g, wo, wr)


def _dispatch_plan(route):
    t = route.shape[0]
    e = route[:, :2].astype(jnp.int32).reshape(-1)
    onehot = (e[:, None] == jnp.arange(N_EXPERTS)[None, :]).astype(jnp.int32)
    rank = jnp.sum((jnp.cumsum(onehot, axis=0) - 1) * onehot, axis=1)
    counts = jnp.sum(onehot, axis=0)
    padded = ((counts + MOE_TILE - 1) // MOE_TILE) * MOE_TILE
    ends = jnp.cumsum(padded)
    starts = ends - padded
    slot = starts[e] + rank
    n_tiles = (2 * t) // MOE_TILE + N_EXPERTS
    tile_id = jnp.arange(n_tiles, dtype=jnp.int32)
    used = ends[-1] // MOE_TILE
    last = jnp.minimum(tile_id, used - 1)
    te = jnp.sum(((ends // MOE_TILE)[None, :] <= last[:, None]).astype(jnp.int32), axis=1)
    valid = (tile_id < used).astype(jnp.int32)
    slot_token = jnp.zeros((n_tiles * MOE_TILE,), jnp.int32).at[slot].set(jnp.arange(2 * t, dtype=jnp.int32) // 2)
    return te, valid, slot_token.reshape(n_tiles, 1, MOE_TILE), slot.reshape(t, 2).astype(jnp.int32)


def _expert_kernel(te_ref, valid_ref, idx_ref, h_hbm, wg_ref, wu_ref, wd_ref, o_ref, hbuf, wgb, wub, wdb, sem):
    i = pl.program_id(0)

    @pl.when(valid_ref[i] == 1)
    def _():
        def issue(r, carry):
            tok = idx_ref[0, 0, r]
            pltpu.make_async_copy(h_hbm.at[pl.ds(tok, 1), :], hbuf.at[pl.ds(r, 1), :], sem).start()
            return carry

        lax.fori_loop(0, MOE_TILE, issue, 0)

        @pl.when((i == 0) | (te_ref[i] != te_ref[jnp.maximum(i - 1, 0)]))
        def _():
            wgb[...] = wg_ref[0, 0].astype(BF16)
            wub[...] = wu_ref[0, 0].astype(BF16)
            wdb[...] = wd_ref[0, 0].astype(BF16)

        pltpu.make_async_copy(h_hbm.at[pl.ds(0, MOE_TILE), :], hbuf, sem).wait()
        hb = hbuf[...].astype(BF16)
        gate = jnp.dot(hb, wgb[...], preferred_element_type=F32)
        up = jnp.dot(hb, wub[...], preferred_element_type=F32)
        hdn = gate * (1.0 / (1.0 + jnp.exp(-gate))) * up
        o_ref[...] = jnp.dot(hdn.astype(BF16), wdb[...], preferred_element_type=F32)

    @pl.when(valid_ref[i] == 0)
    def _():
        o_ref[...] = jnp.zeros_like(o_ref)


def _expert_tiles(te, valid, slot_token, h, w_gate, w_up, w_down, layer):
    n_tiles = slot_token.shape[0]
    grid_spec = pltpu.PrefetchScalarGridSpec(
        num_scalar_prefetch=2,
        grid=(n_tiles,),
        in_specs=[pl.BlockSpec((1, 1, MOE_TILE), lambda i, te, va: (i, 0, 0), memory_space=pltpu.SMEM),
                  pl.BlockSpec(memory_space=pl.ANY),
                  pl.BlockSpec((1, 1, D_MODEL, EXPERT_FF), lambda i, te, va: (layer, te[i], 0, 0)),
                  pl.BlockSpec((1, 1, D_MODEL, EXPERT_FF), lambda i, te, va: (layer, te[i], 0, 0)),
                  pl.BlockSpec((1, 1, EXPERT_FF, D_MODEL), lambda i, te, va: (layer, te[i], 0, 0))],
        out_specs=pl.BlockSpec((MOE_TILE, D_MODEL), lambda i, te, va: (i, 0)),
        scratch_shapes=[pltpu.VMEM((MOE_TILE, D_MODEL), F32),
                        pltpu.VMEM((D_MODEL, EXPERT_FF), BF16), pltpu.VMEM((D_MODEL, EXPERT_FF), BF16),
                        pltpu.VMEM((EXPERT_FF, D_MODEL), BF16), pltpu.SemaphoreType.DMA(())])
    return pl.pallas_call(
        _expert_kernel,
        grid_spec=grid_spec,
        out_shape=jax.ShapeDtypeStruct((n_tiles * MOE_TILE, D_MODEL), F32),
        compiler_params=pltpu.CompilerParams(vmem_limit_bytes=VMEM_LIMIT, dimension_semantics=("arbitrary",)),
        name="moe_experts",
    )(te, valid, slot_token, h, w_gate, w_up, w_down)


def _combine_kernel(slot_ref, x_ref, g2_ref, route_ref, fg_ref, o_hbm, y_ref, obuf, sem, *, tn, final):
    def issue(r, carry):
        for k in range(2):
            s = slot_ref[0, 0, 2 * r + k]
            pltpu.make_async_copy(o_hbm.at[pl.ds(s, 1), :], obuf.at[k, pl.ds(r, 1), :], sem).start()
        return carry

    lax.fori_loop(0, tn, issue, 0)
    for k in range(2):
        pltpu.make_async_copy(o_hbm.at[pl.ds(0, tn), :], obuf.at[k], sem).wait()
    rec = route_ref[0]
    w1 = rec[:, _ROUTE_W1:_ROUTE_W1 + 1]
    w2 = rec[:, _ROUTE_W2:_ROUTE_W2 + 1]
    y = x_ref[0] + g2_ref[0] * (w1 * obuf[0] + w2 * obuf[1])
    if final:
        ms = jnp.mean(y * y, axis=-1, keepdims=True)
        y = y * lax.rsqrt(ms + EPS) * fg_ref[...]
    y_ref[0] = y


def _combine(slots, x, g2, route, final_g, o, *, final, tn=256):
    b, n, _ = x.shape
    nt = n // tn
    slots3 = slots.reshape(b * nt, 1, 2 * tn)
    row = lambda bi, i: (bi, i, 0)
    return pl.pallas_call(
        functools.partial(_combine_kernel, tn=tn, final=final),
        grid=(b, nt),
        in_specs=[pl.BlockSpec((1, 1, 2 * tn), lambda bi, i: (bi * nt + i, 0, 0), memory_space=pltpu.SMEM),
                  pl.BlockSpec((1, tn, D_MODEL), row),
                  pl.BlockSpec((1, 1, D_MODEL), lambda bi, i: (bi, 0, 0)),
                  pl.BlockSpec((1, tn, LANES), row),
                  pl.BlockSpec((1, D_MODEL), lambda bi, i: (0, 0)),
                  pl.BlockSpec(memory_space=pl.ANY)],
        out_specs=pl.BlockSpec((1, tn, D_MODEL), row),
        out_shape=jax.ShapeDtypeStruct((b, n, D_MODEL), F32),
        scratch_shapes=[pltpu.VMEM((2, tn, D_MODEL), F32), pltpu.SemaphoreType.DMA(())],
        compiler_params=pltpu.CompilerParams(vmem_limit_bytes=VMEM_LIMIT),
        name="moe_combine",
    )(slots3, x, g2, route, final_g, o)


def _moe(x, h, route, g2, final_g, w_gate, w_up, w_down, layer, *, final):
    b, n, _ = x.shape
    te, valid, slot_token, slots = _dispatch_plan(route.reshape(b * n, LANES))
    o = _expert_tiles(te, valid, slot_token, h.reshape(b * n, D_MODEL), w_gate, w_up, w_down, layer)
    return _combine(slots, x, g2, route, final_g, o, final=final)


def _layer(x, mod, layer, prm, *, latent, cache=None, rope_tabs=None, final):
    sh1, sc1, g1, sh2, sc2, g2 = mod
    qkv = _qkv_proj(x, sc1, sh1, prm["norm1_g"], prm["w_in"], prm["gains"], rope_tabs if latent else None)
    if latent:
        cache_k, cache_v = cache
        mixes = [_natten(qkv, cache_k, cache_v, layer, prm["na_bias"]),
                 _window(qkv, cache_k, cache_v, layer, prm["sink"]),
                 _dense(qkv, cache_k, cache_v, layer, prm["lam"], prm["subg"], diff=False, post_scale=1.0),
                 _dense(qkv, cache_k, cache_v, layer, prm["lam"], prm["subg"], diff=True,
                        post_scale=prm["post_scale"])]
    else:
        mixes = [_ctx_attention(qkv, prm["sink"], prm["lam"], prm["subg"], prm["post_scale"])]
    x1, h2, route = _oproj_router(mixes, x, g1, sc2, sh2, prm["norm2_g"], prm["w_out"], prm["w_router"])
    y = _moe(x1, h2, route, g2, prm["final_g"], prm["w_gate"], prm["w_up"], prm["w_down"], layer, final=final)
    return y, qkv


def kernel(x_prompt, x_sample, cache_k, cache_v, c, c_ctx, w_ada, b_ada, norm1_g, norm2_g, w_in, w_out, na_rpb,
           sink_logit, qn_g, kn_g, lam_q1, lam_k1, lam_q2, lam_k2, subln_g, w_router_group, w_router_expert,
           w_gate, w_up, w_down, final_g):
    nb, n_ctx, _ = x_prompt.shape
    db, n_lat, _ = x_sample.shape
    past = cache_k.shape[2]
    cond = jnp.concatenate([c_ctx[None, :], c, jnp.zeros((16 - 1 - db, D_MODEL), F32)], axis=0)
    mod = _modulation(cond, w_ada, b_ada)
    ck = cache_k.reshape(db, DEPTH, past, CACHE_WIDTH)
    cv = cache_v.reshape(db, DEPTH, past, CACHE_WIDTH)
    rope_tabs = _rope_tables(n_lat)
    xp, xs = x_prompt, x_sample
    new_k, new_v = [], []
    for l in range(DEPTH):
        lam_init = 0.8 - 0.6 * math.exp(-0.3 * l)
        lam = (jnp.exp(jnp.sum(lam_q1[l] * lam_k1[l])) - jnp.exp(jnp.sum(lam_q2[l] * lam_k2[l])) + lam_init)
        prm = {
            "norm1_g": norm1_g[l][None, :], "norm2_g": norm2_g[l][None, :], "final_g": final_g[None, :],
            "w_in": _permute_w_in(w_in[l]).astype(BF16),
            "w_out": _permute_w_out(w_out[l]).astype(BF16),
            "gains": jnp.stack([jnp.tile(qn_g[l], 2), jnp.tile(kn_g[l], 2)]),
            "na_bias": _natten_bias(na_rpb[l], n_lat // GRID_W),
            "sink": sink_logit[l],
            "lam": lam.reshape(1).astype(F32),
            "subg": jnp.tile(subln_g[l], 2)[None, :],
            "post_scale": 1.0 - lam_init,
            "w_router": jnp.concatenate(
                [w_router_group[l], w_router_expert[l],
                 jnp.zeros((D_MODEL, LANES - N_GROUPS - N_EXPERTS), F32)], axis=1),
            "w_gate": w_gate, "w_up": w_up, "w_down": w_down,
        }
        m = mod[l].reshape(16, 6, 1, D_MODEL)
        mod_ctx = [jnp.broadcast_to(m[0:1, s], (nb, 1, D_MODEL)) for s in range(6)]
        mod_lat = [m[1:1 + db, s] for s in range(6)]
        final = l == DEPTH - 1
        xp, qkv_c = _layer(xp, mod_ctx, l, prm, latent=False, final=final)
        xs, _ = _layer(xs, mod_lat, l, prm, latent=True, cache=(ck, cv), rope_tabs=rope_tabs, final=final)
        new_k.append(jnp.concatenate([qkv_c[..., 256:512], qkv_c[..., 1024:1152], qkv_c[..., 1536:1664],
                                      qkv_c[..., 2048:2304]], axis=-1))
        new_v.append(jnp.concatenate([qkv_c[..., 512:768], qkv_c[..., 1152:1280], qkv_c[..., 1664:1792],
                                      qkv_c[..., 2304:2560]], axis=-1))
    heads = CACHE_WIDTH // HEAD_DIM
    new_cache_k = jnp.stack(new_k, axis=1).astype(F32).reshape(nb, DEPTH, n_ctx, heads, HEAD_DIM)
    new_cache_v = jnp.stack(new_v, axis=1).astype(F32).reshape(nb, DEPTH, n_ctx, heads, HEAD_DIM)
    return (xp, xs, new_cache_k, new_cache_v)
```

```python
import functools
import math

import numpy as np
import jax
import jax.numpy as jnp
from jax import lax
from jax.experimental import pallas as pl
from jax.experimental.pallas import tpu as pltpu

D_MODEL = 1024
DEPTH = 2
GRID_W = 64
HEAD_DIM = 64
D_HALF = HEAD_DIM // 2
NA_ROWS = 8
NA_COLS = 16
WINDOW = 128
ROPE_BASE = 10000.0
N_GROUPS = 4
EXPERTS_PER_GROUP = 8
N_EXPERTS = N_GROUPS * EXPERTS_PER_GROUP
EXPERT_FF = 512
EPS = 1e-6
NEG_INF = -1e30
QKV_WIDTH = 2560
MIX_WIDTH = 1024
CACHE_WIDTH = 768

LANES = 128
MOE_TILE = 512
VMEM_LIMIT = 48 * 1024 * 1024

F32 = jnp.float32
BF16 = jnp.bfloat16

_GQA_ORDER = (0, 2, 1, 3)


def _permute_gqa_heads(w, bases, axis):
    pieces = []
    pos = 0
    for base in bases:
        pieces.append(lax.slice_in_dim(w, pos, base, axis=axis))
        for h in _GQA_ORDER:
            pieces.append(lax.slice_in_dim(w, base + h * HEAD_DIM, base + (h + 1) * HEAD_DIM, axis=axis))
        pos = base + 4 * HEAD_DIM
    pieces.append(lax.slice_in_dim(w, pos, w.shape[axis], axis=axis))
    return jnp.concatenate(pieces, axis=axis)


def _permute_w_in(w):
    return _permute_gqa_heads(w, (768, 1280), 1)


def _permute_w_out(w):
    return _permute_gqa_heads(w, (256, 512), 0)


def _lane(shape):
    return lax.broadcasted_iota(jnp.int32, shape, len(shape) - 1)


def _head_rms(x):
    lo = _lane(x.shape) < HEAD_DIM
    sq = x * x
    s_lo = jnp.sum(jnp.where(lo, sq, 0.0), axis=-1, keepdims=True)
    s_hi = jnp.sum(jnp.where(lo, 0.0, sq), axis=-1, keepdims=True)
    return jnp.where(lo, lax.rsqrt(s_lo * (1.0 / HEAD_DIM) + EPS), lax.rsqrt(s_hi * (1.0 / HEAD_DIM) + EPS))


def _swap_pairs(x):
    n = x.shape[-1]
    even = (_lane(x.shape) % 2) == 0
    return jnp.where(even, pltpu.roll(x, n - 1, axis=x.ndim - 1), pltpu.roll(x, 1, axis=x.ndim - 1))


def _qk(qm, k):
    return lax.dot_general(qm, k, (((1,), (1,)), ((), ())), preferred_element_type=F32)


def _mask_q(q, lo, hi):
    lane = _lane(q.shape)
    return jnp.where((lane >= lo) & (lane < hi), q, jnp.zeros_like(q))


def _joint_softmax_pv(scores, values, extra_logit=None):
    m = scores[0].max(axis=-1, keepdims=True)
    for s in scores[1:]:
        m = jnp.maximum(m, s.max(axis=-1, keepdims=True))
    if extra_logit is not None:
        m = jnp.maximum(m, extra_logit)
    l = None
    o = None
    for s, v in zip(scores, values):
        p = jnp.exp2(s - m)
        ls = p.sum(axis=-1, keepdims=True)
        os_ = jnp.dot(p.astype(BF16), v, preferred_element_type=F32)
        l = ls if l is None else l + ls
        o = os_ if o is None else o + os_
    if extra_logit is not None:
        l = l + jnp.exp2(extra_logit - m)
    return o / l


def _diff_finish(a1_h0, a2_h0, a1_h1, a2_h1, lam, subg, post_scale):
    lo = _lane(a1_h0.shape) < HEAD_DIM
    d = jnp.where(lo, a1_h0 - lam * a2_h0, a1_h1 - lam * a2_h1)
    return d * _head_rms(d) * subg * post_scale


def _mod_kernel(c_ref, w_ref, b_ref, o_ref):
    c = c_ref[...]
    s = c * (1.0 / (1.0 + jnp.exp(-c)))
    o_ref[0] = jnp.dot(s.astype(BF16), w_ref[0].astype(BF16), preferred_element_type=F32) + b_ref[0]


def _modulation(cond, w_ada, b_ada):
    r = cond.shape[0]
    tn = 1536
    return pl.pallas_call(
        _mod_kernel,
        grid=(DEPTH, 6 * D_MODEL // tn),
        in_specs=[pl.BlockSpec((r, D_MODEL), lambda l, j: (0, 0)),
                  pl.BlockSpec((1, D_MODEL, tn), lambda l, j: (l, 0, j)),
                  pl.BlockSpec((1, 1, tn), lambda l, j: (l, 0, j))],
        out_specs=pl.BlockSpec((1, r, tn), lambda l, j: (l, 0, j)),
        out_shape=jax.ShapeDtypeStruct((DEPTH, r, 6 * D_MODEL), F32),
        compiler_params=pltpu.CompilerParams(vmem_limit_bytes=VMEM_LIMIT),
        name="modulation",
    )(cond, w_ada, b_ada.reshape(DEPTH, 1, 6 * D_MODEL))


LOG2E = math.log2(math.e)
_SCALE64 = HEAD_DIM ** -0.5 * LOG2E
_SCALE32 = D_HALF ** -0.5 * LOG2E
_BLOCK_KIND = {0: (None, None, _SCALE64), 1: (None, None, _SCALE64),
               6: (None, 64, _SCALE64), 7: (None, 64, _SCALE64), 8: (None, 64, 1.0),
               10: (0, 64, _SCALE64), 11: (0, 64, _SCALE64), 12: (1, 64, 1.0),
               14: (None, 32, _SCALE32), 15: (None, 32, _SCALE32), 16: (None, 32, 1.0), 17: (None, 32, 1.0)}


def _qkv_kernel(x_ref, sc_ref, sh_ref, g_ref, w_ref, gains_ref, *rest, rope):
    if rope:
        cos64_ref, sin64_ref, cos32_ref, sin32_ref, o_ref = rest
    else:
        (o_ref,) = rest
    x = x_ref[0]
    ms = jnp.mean(x * x, axis=-1, keepdims=True)
    h = (x * lax.rsqrt(ms + EPS) * g_ref[...]) * (1.0 + sc_ref[0]) + sh_ref[0]
    hb = h.astype(BF16)
    for c in range(QKV_WIDTH // 256):
        acc = jnp.dot(hb, w_ref[:, c * 256:(c + 1) * 256], preferred_element_type=F32)
        for half in range(2):
            j = 2 * c + half
            y = acc[:, half * LANES:(half + 1) * LANES]
            gain_row, rope_kind, scale = _BLOCK_KIND.get(j, (None, None, 1.0))
            if gain_row is not None:
                y = y * _head_rms(y) * gains_ref[gain_row:gain_row + 1, :]
            if rope and rope_kind is not None:
                cos_ref, sin_ref = (cos64_ref, sin64_ref) if rope_kind == 64 else (cos32_ref, sin32_ref)
                y = y * cos_ref[...] + _swap_pairs(y) * sin_ref[...]
            if scale != 1.0:
                y = y * scale
            o_ref[0, :, j * LANES:(j + 1) * LANES] = y.astype(BF16)


def _qkv_proj(x, sc, sh, g, w, gains, rope_tabs, tn=256):
    b, n, _ = x.shape
    rope = rope_tabs is not None
    in_specs = [pl.BlockSpec((1, tn, D_MODEL), lambda bi, i: (bi, i, 0)),
                pl.BlockSpec((1, 1, D_MODEL), lambda bi, i: (bi, 0, 0)),
                pl.BlockSpec((1, 1, D_MODEL), lambda bi, i: (bi, 0, 0)),
                pl.BlockSpec((1, D_MODEL), lambda bi, i: (0, 0)),
                pl.BlockSpec((D_MODEL, QKV_WIDTH), lambda bi, i: (0, 0)),
                pl.BlockSpec((2, LANES), lambda bi, i: (0, 0))]
    args = [x, sc, sh, g, w, gains]
    if rope:
        in_specs += [pl.BlockSpec((tn, LANES), lambda bi, i: (i, 0))] * 4
        args += list(rope_tabs)
    return pl.pallas_call(
        functools.partial(_qkv_kernel, rope=rope),
        grid=(b, n // tn),
        in_specs=in_specs,
        out_specs=pl.BlockSpec((1, tn, QKV_WIDTH), lambda bi, i: (bi, i, 0)),
        out_shape=jax.ShapeDtypeStruct((b, n, QKV_WIDTH), BF16),
        compiler_params=pltpu.CompilerParams(vmem_limit_bytes=VMEM_LIMIT),
        name="qkv_rope" if rope else "qkv_ctx",
    )(*args)


def _rope_tables(n):
    t = jnp.arange(n, dtype=jnp.int32)
    row = (t // GRID_W).astype(F32)
    col = (t % GRID_W).astype(F32)
    out = []
    for rot_dim in (HEAD_DIM, D_HALF):
        nf = rot_dim // 4
        inv = ROPE_BASE ** (-jnp.arange(nf, dtype=F32) / nf)
        ang = jnp.concatenate([row[:, None] * inv, col[:, None] * inv], axis=-1)
        cos = jnp.repeat(jnp.cos(ang), 2, axis=-1)
        sin = jnp.repeat(jnp.sin(ang), 2, axis=-1)
        sign = jnp.asarray(np.tile(np.array([-1.0, 1.0], np.float32), rot_dim // 2))
        reps = LANES // rot_dim
        out += [jnp.tile(cos, (1, reps)), jnp.tile(sin * sign, (1, reps))]
    return tuple(out)


def _ctx_attn_kernel(sink_ref, lam_ref, qkv_ref, subg_ref, o_ref, *, post_scale):
    def blk(j):
        return qkv_ref[0, :, j * LANES:(j + 1) * LANES]

    def two_head(qj, kj, vj, sinks):
        q, k, v = blk(qj), blk(kj), blk(vj)
        outs = []
        for mi in range(2):
            s = _qk(_mask_q(q, mi * HEAD_DIM, (mi + 1) * HEAD_DIM), k)
            outs.append(_joint_softmax_pv([s], [v], None if sinks is None else sinks[mi]))
        return jnp.where(_lane(outs[0].shape) < HEAD_DIM, outs[0], outs[1])

    for b2 in range(2):
        o_ref[0, :, b2 * LANES:(b2 + 1) * LANES] = two_head(b2, 2 + b2, 4 + b2, None).astype(BF16)
        o_ref[0, :, (2 + b2) * LANES:(3 + b2) * LANES] = two_head(
            6 + b2, 8, 9, (sink_ref[b2] * LOG2E, sink_ref[b2 + 2] * LOG2E)).astype(BF16)
        o_ref[0, :, (4 + b2) * LANES:(5 + b2) * LANES] = two_head(10 + b2, 12, 13, None).astype(BF16)
        q, k, v = blk(14 + b2), blk(16 + b2), blk(18 + b2)
        a = [_joint_softmax_pv([_qk(_mask_q(q, mi * D_HALF, (mi + 1) * D_HALF), k)], [v]) for mi in range(4)]
        o_ref[0, :, (6 + b2) * LANES:(7 + b2) * LANES] = _diff_finish(
            a[0], a[1], a[2], a[3], lam_ref[0], subg_ref[...], post_scale).astype(BF16)


def _ctx_attention(qkv, sink, lam, subg, post_scale):
    b, n, _ = qkv.shape
    return pl.pallas_call(
        functools.partial(_ctx_attn_kernel, post_scale=post_scale),
        grid=(b,),
        in_specs=[pl.BlockSpec(memory_space=pltpu.SMEM),
                  pl.BlockSpec(memory_space=pltpu.SMEM),
                  pl.BlockSpec((1, n, QKV_WIDTH), lambda bi: (bi, 0, 0)),
                  pl.BlockSpec((1, LANES), lambda bi: (0, 0))],
        out_specs=pl.BlockSpec((1, n, MIX_WIDTH), lambda bi: (bi, 0, 0)),
        out_shape=jax.ShapeDtypeStruct((b, n, MIX_WIDTH), BF16),
        compiler_params=pltpu.CompilerParams(vmem_limit_bytes=VMEM_LIMIT),
        name="ctx_attention",
    )(sink, lam, qkv, subg)


_NA_QROWS = 2
_NA_KBLOCKS = 5
_NA_CASES = (0, 2, 4, 60, 62)


def _natten_bias(rpb, rows):
    assert rows == 64
    nblk = rows // _NA_QROWS
    r0s = np.array(_NA_CASES)
    nq, nk = _NA_QROWS * GRID_W, _NA_KBLOCKS * LANES
    qr = r0s[:, None] + (np.arange(nq) // GRID_W)[None, :]
    qc = np.arange(nq) % GRID_W
    start = np.clip((r0s - NA_ROWS // 2) // 2, 0, nblk - _NA_KBLOCKS)
    kr = 2 * start[:, None] + (np.arange(nk) // GRID_W)[None, :]
    kc = np.arange(nk) % GRID_W
    rs = np.clip(qr - NA_ROWS // 2, 0, rows - NA_ROWS)
    vrow = (kr[:, None, :] >= rs[:, :, None]) & (kr[:, None, :] < rs[:, :, None] + NA_ROWS)
    cs = np.clip(qc - NA_COLS // 2, 0, GRID_W - NA_COLS)
    vcol = (kc[None, :] >= cs[:, None]) & (kc[None, :] < cs[:, None] + NA_COLS)
    valid = vrow & vcol[None]
    w = GRID_W
    nh, na, nc = rpb.shape
    front = (w - 1) - (NA_COLS - 1)
    line = jnp.pad(rpb.astype(F32), ((0, 0), (0, 0), (front, 2 * w - front - nc)))
    toep = jnp.tile(line, (1, 1, w))[:, :, :w * (2 * w - 1)].reshape(nh, na, w, 2 * w - 1)[..., w - 1:]
    qr_b = r0s[:, None] + np.arange(_NA_QROWS)[None, :]
    kr_b = 2 * start[:, None] + np.arange(nk // w)[None, :]
    a_idx = np.clip(kr_b[:, None, :] - qr_b[:, :, None] + (NA_ROWS - 1), 0, na - 1)
    bias = toep[:, a_idx]
    bias = bias.transpose(0, 1, 2, 4, 3, 5).reshape(nh, len(_NA_CASES), nq, nk)
    return jnp.where(valid[None], bias * LOG2E, NEG_INF)


def _natten_kernel(q_ref, k0, k1, k2, k3, k4, v0, v1, v2, v3, v4, kc_ref, vc_ref, bias_ref, o_ref):
    q = q_ref[0]
    k = jnp.concatenate([r[0] for r in (k0, k1, k2, k3, k4)], axis=0)
    v = jnp.concatenate([r[0] for r in (v0, v1, v2, v3, v4)], axis=0)
    kc = kc_ref[0, 0].astype(BF16)
    vc = vc_ref[0, 0].astype(BF16)
    outs = []
    for mi in range(2):
        qm = _mask_q(q, mi * HEAD_DIM, (mi + 1) * HEAD_DIM)
        s_loc = _qk(qm, k) + bias_ref[mi, 0]
        s_ctx = _qk(qm, kc)
        outs.append(_joint_softmax_pv([s_ctx, s_loc], [vc, v]))
    o_ref[0] = jnp.where(_lane(outs[0].shape) < HEAD_DIM, outs[0], outs[1]).astype(BF16)


def _natten(qkv, cache_k, cache_v, layer, bias):
    b, n, _ = qkv.shape
    nblk = n // LANES

    def start(i):
        return jnp.clip(i - 2, 0, nblk - _NA_KBLOCKS)

    def case(i):
        return jnp.where(i == 0, 0, jnp.where(i == 1, 1, jnp.where(i == nblk - 2, 3, jnp.where(i == nblk - 1, 4, 2))))

    def kv_spec(colblk, jj):
        return pl.BlockSpec((1, LANES, LANES), lambda bi, j, i: (bi, start(i) + jj, colblk + j))

    in_specs = ([pl.BlockSpec((1, LANES, LANES), lambda bi, j, i: (bi, i, j))]
                + [kv_spec(2, jj) for jj in range(_NA_KBLOCKS)]
                + [kv_spec(4, jj) for jj in range(_NA_KBLOCKS)]
                + [pl.BlockSpec((1, 1, cache_k.shape[2], LANES), lambda bi, j, i: (bi, layer, 0, j))] * 2
                + [pl.BlockSpec((2, 1, LANES, _NA_KBLOCKS * LANES), lambda bi, j, i: (j, case(i), 0, 0))])
    return pl.pallas_call(
        _natten_kernel,
        grid=(b, 2, nblk),
        in_specs=in_specs,
        out_specs=pl.BlockSpec((1, LANES, LANES), lambda bi, j, i: (bi, i, j)),
        out_shape=jax.ShapeDtypeStruct((b, n, 2 * LANES), BF16),
        compiler_params=pltpu.CompilerParams(vmem_limit_bytes=VMEM_LIMIT),
        name="natten",
    )(qkv, *([qkv] * (2 * _NA_KBLOCKS)), cache_k, cache_v, bias)


def _window_kernel(sink_ref, q_ref, k0, k1, k2, v0, v1, v2, kc_ref, vc_ref, o_ref, *, n):
    j = pl.program_id(1)
    i = pl.program_id(2)
    q = q_ref[0]
    k = jnp.concatenate([r[0] for r in (k0, k1, k2)], axis=0)
    v = jnp.concatenate([r[0] for r in (v0, v1, v2)], axis=0)
    kc = kc_ref[0, 0].astype(BF16)
    vc = vc_ref[0, 0].astype(BF16)
    r = lax.broadcasted_iota(jnp.int32, (LANES, 3 * LANES), 0)
    c = lax.broadcasted_iota(jnp.int32, (LANES, 3 * LANES), 1)
    kp = (i - 1) * LANES + c
    ok = (kp >= 0) & (kp < n) & (jnp.abs(c - LANES - r) <= WINDOW)
    outs = []
    for mi in range(2):
        qm = _mask_q(q, mi * HEAD_DIM, (mi + 1) * HEAD_DIM)
        s_loc = jnp.where(ok, _qk(qm, k), NEG_INF)
        s_ctx = _qk(qm, kc)
        outs.append(_joint_softmax_pv([s_ctx, s_loc], [vc, v], sink_ref[j + 2 * mi] * LOG2E))
    o_ref[0] = jnp.where(_lane(outs[0].shape) < HEAD_DIM, outs[0], outs[1]).astype(BF16)


def _window(qkv, cache_k, cache_v, layer, sink):
    b, n, _ = qkv.shape
    nblk = n // LANES

    def kv_spec(colblk, jj):
        return pl.BlockSpec((1, LANES, LANES), lambda bi, j, i: (bi, jnp.clip(i - 1 + jj, 0, nblk - 1), colblk))

    in_specs = ([pl.BlockSpec(memory_space=pltpu.SMEM),
                 pl.BlockSpec((1, LANES, LANES), lambda bi, j, i: (bi, i, 6 + j))]
                + [kv_spec(8, jj) for jj in range(3)]
                + [kv_spec(9, jj) for jj in range(3)]
                + [pl.BlockSpec((1, 1, cache_k.shape[2], LANES), lambda bi, j, i: (bi, layer, 0, 2))] * 2)
    return pl.pallas_call(
        functools.partial(_window_kernel, n=n),
        grid=(b, 2, nblk),
        in_specs=in_specs,
        out_specs=pl.BlockSpec((1, LANES, LANES), lambda bi, j, i: (bi, i, j)),
        out_shape=jax.ShapeDtypeStruct((b, n, 2 * LANES), BF16),
        compiler_params=pltpu.CompilerParams(vmem_limit_bytes=VMEM_LIMIT),
        name="window",
    )(sink, qkv, *([qkv] * 6), cache_k, cache_v)


_DENSE_TK = 512
_DENSE_STRIP = 128


def _dense_kernel(lam_ref, q_ref, k_ref, v_ref, kc_ref, vc_ref, subg_ref, o_ref,
                  qs_ref, kcb, vcb, s0, s1, p0, p1, l0, l1, *, n_maps, tq, diff, post_scale):
    n = k_ref.shape[1]
    nq = n // tq
    rows = n_maps * tq
    width = LANES // n_maps
    n_chunks = n // _DENSE_TK
    kcb[...] = kc_ref[0, 0].astype(BF16)
    vcb[...] = vc_ref[0, 0].astype(BF16)
    n_ctx = kcb.shape[0]

    def scores(i, s_ref):
        q = q_ref[0, pl.ds(pl.multiple_of(i * tq, tq), tq), :]
        for mi in range(n_maps):
            qs_ref[mi * tq:(mi + 1) * tq, :] = _mask_q(q, mi * width, (mi + 1) * width)
        qs = qs_ref[...]
        for c in range(n_chunks):
            s_ref[:, c * _DENSE_TK:(c + 1) * _DENSE_TK] = _qk(qs, k_ref[0, c * _DENSE_TK:(c + 1) * _DENSE_TK, :])
        s_ref[:, n:] = _qk(qs, kcb[...])

    def probs(s_ref, p_ref, l_ref):
        n_tiles = (n + n_ctx) // LANES
        for r0 in range(0, rows, _DENSE_STRIP):
            mx = None
            for t in range(n_tiles):
                piece = s_ref[r0:r0 + _DENSE_STRIP, t * LANES:(t + 1) * LANES]
                mx = piece if mx is None else jnp.maximum(mx, piece)
            m = jnp.broadcast_to(mx.max(axis=-1, keepdims=True), (_DENSE_STRIP, LANES))
            ls = None
            for t in range(n_tiles):
                p = jnp.exp2(s_ref[r0:r0 + _DENSE_STRIP, t * LANES:(t + 1) * LANES] - m)
                ls = p if ls is None else ls + p
                p_ref[r0:r0 + _DENSE_STRIP, t * LANES:(t + 1) * LANES] = p.astype(BF16)
            l_ref[r0:r0 + _DENSE_STRIP, :] = ls

    def values(i, p_ref, l_ref):
        o = (jnp.dot(p_ref[:, :n], v_ref[0], preferred_element_type=F32)
             + jnp.dot(p_ref[:, n:], vcb[...], preferred_element_type=F32))
        a = o / l_ref[...].sum(axis=-1, keepdims=True)
        parts = [a[mi * tq:(mi + 1) * tq, :] for mi in range(n_maps)]
        if diff:
            out = _diff_finish(parts[0], parts[1], parts[2], parts[3], lam_ref[0], subg_ref[...], post_scale)
        else:
            out = jnp.where(_lane(parts[0].shape) < HEAD_DIM, parts[0], parts[1])
        o_ref[0, pl.ds(pl.multiple_of(i * tq, tq), tq), :] = out.astype(BF16)

    scores(0, s0)
    scores(1, s1)
    probs(s0, p0, l0)

    def pair(t, carry):
        i = 2 * t
        scores(i, s0)
        probs(s1, p1, l1)
        values(i - 2, p0, l0)
        scores(i + 1, s1)
        probs(s0, p0, l0)
        values(i - 1, p1, l1)
        return carry

    lax.fori_loop(1, nq // 2, pair, 0)
    probs(s1, p1, l1)
    values(nq - 2, p0, l0)
    values(nq - 1, p1, l1)


def _dense(qkv, cache_k, cache_v, layer, lam, subg, *, diff, post_scale, rows=512):
    b, n, _ = qkv.shape
    n_maps = 4 if diff else 2
    tq = rows // n_maps
    assert n % (2 * tq) == 0 and n % _DENSE_TK == 0
    if diff:
        qb, kb, vb, cb = 14, 16, 18, 4
        kmap = lambda bi, j: (bi, 0, kb + j)
        vmap = lambda bi, j: (bi, 0, vb + j)
        cmap = lambda bi, j: (bi, layer, 0, cb + j)
    else:
        qb, kb, vb, cb = 10, 12, 13, 3
        kmap = lambda bi, j: (bi, 0, kb)
        vmap = lambda bi, j: (bi, 0, vb)
        cmap = lambda bi, j: (bi, layer, 0, cb)
    m = cache_k.shape[2]
    keys = n + m
    return pl.pallas_call(
        functools.partial(_dense_kernel, n_maps=n_maps, tq=tq, diff=diff, post_scale=post_scale),
        grid=(b, 2),
        in_specs=[pl.BlockSpec(memory_space=pltpu.SMEM),
                  pl.BlockSpec((1, n, LANES), lambda bi, j: (bi, 0, qb + j)),
                  pl.BlockSpec((1, n, LANES), kmap),
                  pl.BlockSpec((1, n, LANES), vmap),
                  pl.BlockSpec((1, 1, m, LANES), cmap),
                  pl.BlockSpec((1, 1, m, LANES), cmap),
                  pl.BlockSpec((1, LANES), lambda bi, j: (0, 0))],
        out_specs=pl.BlockSpec((1, n, LANES), lambda bi, j: (bi, 0, j)),
        out_shape=jax.ShapeDtypeStruct((b, n, 2 * LANES), BF16),
        scratch_shapes=[pltpu.VMEM((rows, LANES), BF16), pltpu.VMEM((m, LANES), BF16), pltpu.VMEM((m, LANES), BF16),
                        pltpu.VMEM((rows, keys), F32), pltpu.VMEM((rows, keys), F32),
                        pltpu.VMEM((rows, keys), BF16), pltpu.VMEM((rows, keys), BF16),
                        pltpu.VMEM((rows, LANES), F32), pltpu.VMEM((rows, LANES), F32)],
        compiler_params=pltpu.CompilerParams(vmem_limit_bytes=56 * 1024 * 1024),
        name="diff_attn" if diff else "dense_attn",
    )(lam, qkv, qkv, qkv, cache_k, cache_v, subg)


_ROUTE_E1, _ROUTE_E2, _ROUTE_W1, _ROUTE_W2 = 0, 1, 2, 3


def _oproj_kernel(*refs, widths):
    nm = len(widths)
    mix_refs = refs[:nm]
    x_ref, g1_ref, sc_ref, sh_ref, g_ref, wo_ref, wr_ref, xo_ref, h_ref, route_ref = refs[nm:]
    acc = None
    off = 0
    for mref, wd in zip(mix_refs, widths):
        part = jnp.dot(mref[0], wo_ref[off:off + wd, :], preferred_element_type=F32)
        acc = part if acc is None else acc + part
        off += wd
    x = x_ref[0] + g1_ref[0] * acc
    xo_ref[0] = x
    ms = jnp.mean(x * x, axis=-1, keepdims=True)
    h = (x * lax.rsqrt(ms + EPS) * g_ref[...]) * (1.0 + sc_ref[0]) + sh_ref[0]
    h_ref[0] = h
    logits = jnp.dot(h, wr_ref[...], preferred_element_type=F32, precision=lax.Precision.HIGHEST)
    lane = _lane(logits.shape).astype(F32)
    big = float(LANES)
    glog = jnp.where(lane < N_GROUPS, logits, -jnp.inf)
    gmax = glog.max(axis=-1, keepdims=True)
    gsel = jnp.where(glog == gmax, lane, big).min(axis=-1, keepdims=True)
    pg = 1.0 / jnp.exp(glog - gmax).sum(axis=-1, keepdims=True)
    e_lo = N_GROUPS + EXPERTS_PER_GROUP * gsel
    el = jnp.where((lane >= e_lo) & (lane < e_lo + EXPERTS_PER_GROUP), logits, -jnp.inf)
    v1 = el.max(axis=-1, keepdims=True)
    i1 = jnp.where(el == v1, lane, big).min(axis=-1, keepdims=True)
    el2 = jnp.where(lane == i1, -jnp.inf, el)
    v2 = el2.max(axis=-1, keepdims=True)
    i2 = jnp.where(el2 == v2, lane, big).min(axis=-1, keepdims=True)
    t = jnp.exp(v2 - v1)
    w1 = pg / (1.0 + t)
    w2 = pg * t / (1.0 + t)
    rec = jnp.where(lane == _ROUTE_E1, i1 - N_GROUPS,
                    jnp.where(lane == _ROUTE_E2, i2 - N_GROUPS,
                              jnp.where(lane == _ROUTE_W1, w1, jnp.where(lane == _ROUTE_W2, w2, 0.0))))
    route_ref[0] = rec


def _oproj_router(mixes, x, g1, sc2, sh2, g, wo, wr, tn=256):
    b, n, _ = x.shape
    widths = tuple(m.shape[-1] for m in mixes)
    row = lambda bi, i: (bi, i, 0)
    per_b = lambda bi, i: (bi, 0, 0)
    const = lambda bi, i: (0, 0)
    in_specs = ([pl.BlockSpec((1, tn, wd), row) for wd in widths]
                + [pl.BlockSpec((1, tn, D_MODEL), row)]
                + [pl.BlockSpec((1, 1, D_MODEL), per_b)] * 3
                + [pl.BlockSpec((1, D_MODEL), const),
                   pl.BlockSpec((MIX_WIDTH, D_MODEL), const),
                   pl.BlockSpec((D_MODEL, LANES), const)])
    return pl.pallas_call(
        functools.partial(_oproj_kernel, widths=widths),
        grid=(b, n // tn),
        in_specs=in_specs,
        out_specs=[pl.BlockSpec((1, tn, D_MODEL), row), pl.BlockSpec((1, tn, D_MODEL), row),
                   pl.BlockSpec((1, tn, LANES), row)],
        out_shape=[jax.ShapeDtypeStruct((b, n, D_MODEL), F32), jax.ShapeDtypeStruct((b, n, D_MODEL), F32),
                   jax.ShapeDtypeStruct((b, n, LANES), F32)],
        compiler_params=pltpu.CompilerParams(vmem_limit_bytes=VMEM_LIMIT),
        name="oproj_router",
    )(*mixes, x, g1, sc2, sh2, g, wo, wr)


_RANK_BLOCK = 512


def _dispatch_plan(route):
    t = route.shape[0]
    e = route[:, :2].astype(jnp.int32).reshape(-1)
    nb = (2 * t) // _RANK_BLOCK
    onehot = (e[:, None] == jnp.arange(N_EXPERTS)[None, :]).reshape(nb, _RANK_BLOCK, N_EXPERTS)
    tri = jnp.asarray(np.tril(np.ones((_RANK_BLOCK, _RANK_BLOCK), np.float32), -1), BF16)
    within = jnp.einsum("ij,bjk->bik", tri, onehot.astype(BF16), preferred_element_type=F32)
    oh = onehot.astype(F32)
    block_counts = jnp.sum(oh, axis=1)
    block_off = jnp.cumsum(block_counts, axis=0) - block_counts
    counts = jnp.sum(block_counts, axis=0).astype(jnp.int32)
    padded = ((counts + MOE_TILE - 1) // MOE_TILE) * MOE_TILE
    ends = jnp.cumsum(padded)
    starts = (ends - padded).astype(F32)
    slot = jnp.sum(oh * (within + block_off[:, None, :] + starts[None, None, :]), axis=-1)
    n_tiles = (2 * t) // MOE_TILE + N_EXPERTS
    tile_id = jnp.arange(n_tiles, dtype=jnp.int32)
    used = ends[-1] // MOE_TILE
    last = jnp.minimum(tile_id, used - 1)
    te = jnp.sum(((ends // MOE_TILE)[None, :] <= last[:, None]).astype(jnp.int32), axis=1)
    valid = (tile_id < used).astype(jnp.int32)
    return te, valid, slot.reshape(t, 2).astype(jnp.int32)


def _dispatch_kernel(slot_ref, h_hbm, init_hbm, hs_hbm, sems, *, tn, n_steps):
    del init_hbm
    g = pl.program_id(0)
    par = g % 2

    def issue(r, carry):
        for k in range(2):
            s = slot_ref[0, 0, 2 * r + k]
            pltpu.make_async_copy(h_hbm.at[pl.ds(g * tn + r, 1), :], hs_hbm.at[pl.ds(s, 1), :], sems.at[par]).start()
        return carry

    lax.fori_loop(0, tn, issue, 0, unroll=4)

    def wait_step(p):
        pltpu.make_async_copy(h_hbm.at[pl.ds(0, 2 * tn), :], hs_hbm.at[pl.ds(0, 2 * tn), :], sems.at[p]).wait()

    @pl.when(g > 0)
    def _():
        wait_step(1 - par)

    @pl.when(g == n_steps - 1)
    def _():
        wait_step(par)


def _dispatch_rows(slots, h, n_slots, tn=256):
    t = h.shape[0]
    n_steps = t // tn
    return pl.pallas_call(
        functools.partial(_dispatch_kernel, tn=tn, n_steps=n_steps),
        grid=(n_steps,),
        in_specs=[pl.BlockSpec((1, 1, 2 * tn), lambda g: (g, 0, 0), memory_space=pltpu.SMEM),
                  pl.BlockSpec(memory_space=pl.ANY),
                  pl.BlockSpec(memory_space=pl.ANY)],
        out_specs=pl.BlockSpec(memory_space=pl.ANY),
        out_shape=jax.ShapeDtypeStruct((n_slots, D_MODEL), F32),
        input_output_aliases={2: 0},
        scratch_shapes=[pltpu.SemaphoreType.DMA((2,))],
        compiler_params=pltpu.CompilerParams(dimension_semantics=("arbitrary",)),
        name="moe_dispatch",
    )(slots.reshape(n_steps, 1, 2 * tn), h, jnp.zeros((n_slots, D_MODEL), F32))


def _expert_kernel(te_ref, valid_ref, hs_ref, wg_ref, wu_ref, wd_ref, o_ref, wgb, wub, wdb):
    i = pl.program_id(0)

    @pl.when(valid_ref[i] == 1)
    def _():
        @pl.when((i == 0) | (te_ref[i] != te_ref[jnp.maximum(i - 1, 0)]))
        def _():
            wgb[...] = wg_ref[0, 0].astype(BF16)
            wub[...] = wu_ref[0, 0].astype(BF16)
            wdb[...] = wd_ref[0, 0].astype(BF16)

        hb = hs_ref[...].astype(BF16)
        gate = jnp.dot(hb, wgb[...], preferred_element_type=F32)
        up = jnp.dot(hb, wub[...], preferred_element_type=F32)
        hdn = gate * (1.0 / (1.0 + jnp.exp(-gate))) * up
        o_ref[...] = jnp.dot(hdn.astype(BF16), wdb[...], preferred_element_type=F32)

    @pl.when(valid_ref[i] == 0)
    def _():
        o_ref[...] = jnp.zeros_like(o_ref)


def _expert_tiles(te, valid, hs, w_gate, w_up, w_down, layer):
    n_tiles = hs.shape[0] // MOE_TILE
    grid_spec = pltpu.PrefetchScalarGridSpec(
        num_scalar_prefetch=2,
        grid=(n_tiles,),
        in_specs=[pl.BlockSpec((MOE_TILE, D_MODEL), lambda i, te, va: (i, 0)),
                  pl.BlockSpec((1, 1, D_MODEL, EXPERT_FF), lambda i, te, va: (layer, te[i], 0, 0)),
                  pl.BlockSpec((1, 1, D_MODEL, EXPERT_FF), lambda i, te, va: (layer, te[i], 0, 0)),
                  pl.BlockSpec((1, 1, EXPERT_FF, D_MODEL), lambda i, te, va: (layer, te[i], 0, 0))],
        out_specs=pl.BlockSpec((MOE_TILE, D_MODEL), lambda i, te, va: (i, 0)),
        scratch_shapes=[pltpu.VMEM((D_MODEL, EXPERT_FF), BF16), pltpu.VMEM((D_MODEL, EXPERT_FF), BF16),
                        pltpu.VMEM((EXPERT_FF, D_MODEL), BF16)])
    return pl.pallas_call(
        _expert_kernel,
        grid_spec=grid_spec,
        out_shape=jax.ShapeDtypeStruct((n_tiles * MOE_TILE, D_MODEL), F32),
        compiler_params=pltpu.CompilerParams(vmem_limit_bytes=VMEM_LIMIT, dimension_semantics=("arbitrary",)),
        name="moe_experts",
    )(te, valid, hs, w_gate, w_up, w_down)


def _combine_kernel(slot_ref, next_slot_ref, x_ref, g2_ref, route_ref, fg_ref, o_hbm, y_ref, obuf, sems,
                    *, tn, n_steps, final):
    g = pl.program_id(0)
    par = g % 2

    def issue(idx_ref, p):
        def body(r, carry):
            for k in range(2):
                s = idx_ref[0, 0, 2 * r + k]
                pltpu.make_async_copy(o_hbm.at[pl.ds(s, 1), :], obuf.at[p, k, pl.ds(r, 1), :], sems.at[p]).start()
            return carry

        lax.fori_loop(0, tn, body, 0, unroll=4)

    @pl.when(g == 0)
    def _():
        issue(slot_ref, 0)

    @pl.when(g + 1 < n_steps)
    def _():
        issue(next_slot_ref, 1 - par)

    for k in range(2):
        pltpu.make_async_copy(o_hbm.at[pl.ds(0, tn), :], obuf.at[par, k], sems.at[par]).wait()
    rec = route_ref[0]
    w1 = rec[:, _ROUTE_W1:_ROUTE_W1 + 1]
    w2 = rec[:, _ROUTE_W2:_ROUTE_W2 + 1]
    y = x_ref[0] + g2_ref[0] * (w1 * obuf[par, 0] + w2 * obuf[par, 1])
    if final:
        ms = jnp.mean(y * y, axis=-1, keepdims=True)
        y = y * lax.rsqrt(ms + EPS) * fg_ref[...]
    y_ref[0] = y


def _combine(slots, x, g2, route, final_g, o, *, final, tn=256):
    b, n, _ = x.shape
    nt = n // tn
    n_steps = b * nt
    slots3 = slots.reshape(n_steps, 1, 2 * tn)
    row = lambda g: (g // nt, g % nt, 0)
    return pl.pallas_call(
        functools.partial(_combine_kernel, tn=tn, n_steps=n_steps, final=final),
        grid=(n_steps,),
        in_specs=[pl.BlockSpec((1, 1, 2 * tn), lambda g: (g, 0, 0), memory_space=pltpu.SMEM),
                  pl.BlockSpec((1, 1, 2 * tn), lambda g: (jnp.minimum(g + 1, n_steps - 1), 0, 0),
                               memory_space=pltpu.SMEM),
                  pl.BlockSpec((1, tn, D_MODEL), row),
                  pl.BlockSpec((1, 1, D_MODEL), lambda g: (g // nt, 0, 0)),
                  pl.BlockSpec((1, tn, LANES), row),
                  pl.BlockSpec((1, D_MODEL), lambda g: (0, 0)),
                  pl.BlockSpec(memory_space=pl.ANY)],
        out_specs=pl.BlockSpec((1, tn, D_MODEL), row),
        out_shape=jax.ShapeDtypeStruct((b, n, D_MODEL), F32),
        scratch_shapes=[pltpu.VMEM((2, 2, tn, D_MODEL), F32), pltpu.SemaphoreType.DMA((2,))],
        compiler_params=pltpu.CompilerParams(vmem_limit_bytes=VMEM_LIMIT, dimension_semantics=("arbitrary",)),
        name="moe_combine",
    )(slots3, slots3, x, g2, route, final_g, o)


def _moe(x, h, route, g2, final_g, w_gate, w_up, w_down, layer, *, final):
    b, n, _ = x.shape
    te, valid, slots = _dispatch_plan(route.reshape(b * n, LANES))
    hs = _dispatch_rows(slots, h.reshape(b * n, D_MODEL), te.shape[0] * MOE_TILE)
    o = _expert_tiles(te, valid, hs, w_gate, w_up, w_down, layer)
    return _combine(slots, x, g2, route, final_g, o, final=final)


def _layer(x, mod, layer, prm, *, latent, cache=None, rope_tabs=None, final):
    sh1, sc1, g1, sh2, sc2, g2 = mod
    qkv = _qkv_proj(x, sc1, sh1, prm["norm1_g"], prm["w_in"], prm["gains"], rope_tabs if latent else None)
    if latent:
        cache_k, cache_v = cache
        mixes = [_natten(qkv, cache_k, cache_v, layer, prm["na_bias"]),
                 _window(qkv, cache_k, cache_v, layer, prm["sink"]),
                 _dense(qkv, cache_k, cache_v, layer, prm["lam"], prm["subg"], diff=False, post_scale=1.0),
                 _dense(qkv, cache_k, cache_v, layer, prm["lam"], prm["subg"], diff=True,
                        post_scale=prm["post_scale"])]
    else:
        mixes = [_ctx_attention(qkv, prm["sink"], prm["lam"], prm["subg"], prm["post_scale"])]
    x1, h2, route = _oproj_router(mixes, x, g1, sc2, sh2, prm["norm2_g"], prm["w_out"], prm["w_router"])
    y = _moe(x1, h2, route, g2, prm["final_g"], prm["w_gate"], prm["w_up"], prm["w_down"], layer, final=final)
    return y, qkv


def kernel(x_prompt, x_sample, cache_k, cache_v, c, c_ctx, w_ada, b_ada, norm1_g, norm2_g, w_in, w_out, na_rpb,
           sink_logit, qn_g, kn_g, lam_q1, lam_k1, lam_q2, lam_k2, subln_g, w_router_group, w_router_expert,
           w_gate, w_up, w_down, final_g):
    nb, n_ctx, _ = x_prompt.shape
    db, n_lat, _ = x_sample.shape
    past = cache_k.shape[2]
    cond = jnp.concatenate([c_ctx[None, :], c, jnp.zeros((16 - 1 - db, D_MODEL), F32)], axis=0)
    mod = _modulation(cond, w_ada, b_ada)
    ck = cache_k.reshape(db, DEPTH, past, CACHE_WIDTH)
    cv = cache_v.reshape(db, DEPTH, past, CACHE_WIDTH)
    rope_tabs = _rope_tables(n_lat)
    xp, xs = x_prompt, x_sample
    new_k, new_v = [], []
    for l in range(DEPTH):
        lam_init = 0.8 - 0.6 * math.exp(-0.3 * l)
        lam = (jnp.exp(jnp.sum(lam_q1[l] * lam_k1[l])) - jnp.exp(jnp.sum(lam_q2[l] * lam_k2[l])) + lam_init)
        prm = {
            "norm1_g": norm1_g[l][None, :], "norm2_g": norm2_g[l][None, :], "final_g": final_g[None, :],
            "w_in": _permute_w_in(w_in[l]).astype(BF16),
            "w_out": _permute_w_out(w_out[l]).astype(BF16),
            "gains": jnp.stack([jnp.tile(qn_g[l], 2), jnp.tile(kn_g[l], 2)]),
            "na_bias": _natten_bias(na_rpb[l], n_lat // GRID_W),
            "sink": sink_logit[l],
            "lam": lam.reshape(1).astype(F32),
            "subg": jnp.tile(subln_g[l], 2)[None, :],
            "post_scale": 1.0 - lam_init,
            "w_router": jnp.concatenate(
                [w_router_group[l], w_router_expert[l],
                 jnp.zeros((D_MODEL, LANES - N_GROUPS - N_EXPERTS), F32)], axis=1),
            "w_gate": w_gate, "w_up": w_up, "w_down": w_down,
        }
        m = mod[l].reshape(16, 6, 1, D_MODEL)
        mod_ctx = [jnp.broadcast_to(m[0:1, s], (nb, 1, D_MODEL)) for s in range(6)]
        mod_lat = [m[1:1 + db, s] for s in range(6)]
        final = l == DEPTH - 1
        xp, qkv_c = _layer(xp, mod_ctx, l, prm, latent=False, final=final)
        xs, _ = _layer(xs, mod_lat, l, prm, latent=True, cache=(ck, cv), rope_tabs=rope_tabs, final=final)
        new_k.append(jnp.concatenate([qkv_c[..., 256:512], qkv_c[..., 1024:1152], qkv_c[..., 1536:1664],
                                      qkv_c[..., 2048:2304]], axis=-1))
        new_v.append(jnp.concatenate([qkv_c[..., 512:768], qkv_c[..., 1152:1280], qkv_c[..., 1664:1792],
                                      qkv_c[..., 2304:2560]], axis=-1))
    heads = CACHE_WIDTH // HEAD_DIM
    new_cache_k = jnp.stack(new_k, axis=1).astype(F32).reshape(nb, DEPTH, n_ctx, heads, HEAD_DIM)
    new_cache_v = jnp.stack(new_v, axis=1).astype(F32).reshape(nb, DEPTH, n_ctx, heads, HEAD_DIM)
    return (xp, xs, new_cache_k, new_cache_v)
```

```python
import functools
import math

import numpy as np
import jax
import jax.numpy as jnp
from jax import lax
from jax.experimental import pallas as pl
from jax.experimental.pallas import tpu as pltpu

D_MODEL = 1024
DEPTH = 2
GRID_W = 64
HEAD_DIM = 64
D_HALF = HEAD_DIM // 2
NA_ROWS = 8
NA_COLS = 16
WINDOW = 128
ROPE_BASE = 10000.0
N_GROUPS = 4
EXPERTS_PER_GROUP = 8
N_EXPERTS = N_GROUPS * EXPERTS_PER_GROUP
EXPERT_FF = 512
EPS = 1e-6
NEG_INF = -1e30
QKV_WIDTH = 2560
MIX_WIDTH = 1024
CACHE_WIDTH = 768

LANES = 128
MOE_TILE = 512
VMEM_LIMIT = 48 * 1024 * 1024

F32 = jnp.float32
BF16 = jnp.bfloat16

_GQA_ORDER = (0, 2, 1, 3)


def _permute_gqa_heads(w, bases, axis):
    pieces = []
    pos = 0
    for base in bases:
        pieces.append(lax.slice_in_dim(w, pos, base, axis=axis))
        for h in _GQA_ORDER:
            pieces.append(lax.slice_in_dim(w, base + h * HEAD_DIM, base + (h + 1) * HEAD_DIM, axis=axis))
        pos = base + 4 * HEAD_DIM
    pieces.append(lax.slice_in_dim(w, pos, w.shape[axis], axis=axis))
    return jnp.concatenate(pieces, axis=axis)


def _permute_w_in(w):
    return _permute_gqa_heads(w, (768, 1280), 1)


def _permute_w_out(w):
    return _permute_gqa_heads(w, (256, 512), 0)


def _lane(shape):
    return lax.broadcasted_iota(jnp.int32, shape, len(shape) - 1)


def _head_rms(x):
    lo = _lane(x.shape) < HEAD_DIM
    sq = x * x
    s_lo = jnp.sum(jnp.where(lo, sq, 0.0), axis=-1, keepdims=True)
    s_hi = jnp.sum(jnp.where(lo, 0.0, sq), axis=-1, keepdims=True)
    return jnp.where(lo, lax.rsqrt(s_lo * (1.0 / HEAD_DIM) + EPS), lax.rsqrt(s_hi * (1.0 / HEAD_DIM) + EPS))


def _swap_pairs(x):
    n = x.shape[-1]
    even = (_lane(x.shape) % 2) == 0
    return jnp.where(even, pltpu.roll(x, n - 1, axis=x.ndim - 1), pltpu.roll(x, 1, axis=x.ndim - 1))


def _qk(qm, k):
    return lax.dot_general(qm, k, (((1,), (1,)), ((), ())), preferred_element_type=F32)


def _mask_q(q, lo, hi):
    lane = _lane(q.shape)
    return jnp.where((lane >= lo) & (lane < hi), q, jnp.zeros_like(q))


def _joint_softmax_pv(scores, values, extra_logit=None):
    m = scores[0].max(axis=-1, keepdims=True)
    for s in scores[1:]:
        m = jnp.maximum(m, s.max(axis=-1, keepdims=True))
    if extra_logit is not None:
        m = jnp.maximum(m, extra_logit)
    l = None
    o = None
    for s, v in zip(scores, values):
        p = jnp.exp2(s - m)
        ls = p.sum(axis=-1, keepdims=True)
        os_ = jnp.dot(p.astype(BF16), v, preferred_element_type=F32)
        l = ls if l is None else l + ls
        o = os_ if o is None else o + os_
    if extra_logit is not None:
        l = l + jnp.exp2(extra_logit - m)
    return o / l


def _diff_finish(a1_h0, a2_h0, a1_h1, a2_h1, lam, subg, post_scale):
    lo = _lane(a1_h0.shape) < HEAD_DIM
    d = jnp.where(lo, a1_h0 - lam * a2_h0, a1_h1 - lam * a2_h1)
    return d * _head_rms(d) * subg * post_scale


def _mod_kernel(c_ref, w_ref, b_ref, o_ref):
    c = c_ref[...]
    s = c * (1.0 / (1.0 + jnp.exp(-c)))
    o_ref[0] = jnp.dot(s.astype(BF16), w_ref[0].astype(BF16), preferred_element_type=F32) + b_ref[0]


def _modulation(cond, w_ada, b_ada):
    r = cond.shape[0]
    tn = 1536
    return pl.pallas_call(
        _mod_kernel,
        grid=(DEPTH, 6 * D_MODEL // tn),
        in_specs=[pl.BlockSpec((r, D_MODEL), lambda l, j: (0, 0)),
                  pl.BlockSpec((1, D_MODEL, tn), lambda l, j: (l, 0, j)),
                  pl.BlockSpec((1, 1, tn), lambda l, j: (l, 0, j))],
        out_specs=pl.BlockSpec((1, r, tn), lambda l, j: (l, 0, j)),
        out_shape=jax.ShapeDtypeStruct((DEPTH, r, 6 * D_MODEL), F32),
        compiler_params=pltpu.CompilerParams(vmem_limit_bytes=VMEM_LIMIT),
        name="modulation",
    )(cond, w_ada, b_ada.reshape(DEPTH, 1, 6 * D_MODEL))


LOG2E = math.log2(math.e)
_SCALE64 = HEAD_DIM ** -0.5 * LOG2E
_SCALE32 = D_HALF ** -0.5 * LOG2E
_BLOCK_KIND = {0: (None, None, _SCALE64), 1: (None, None, _SCALE64),
               6: (None, 64, _SCALE64), 7: (None, 64, _SCALE64), 8: (None, 64, 1.0),
               10: (0, 64, _SCALE64), 11: (0, 64, _SCALE64), 12: (1, 64, 1.0),
               14: (None, 32, _SCALE32), 15: (None, 32, _SCALE32), 16: (None, 32, 1.0), 17: (None, 32, 1.0)}


def _qkv_kernel(x_ref, sc_ref, sh_ref, g_ref, w_ref, gains_ref, *rest, rope):
    if rope:
        cos64_ref, sin64_ref, cos32_ref, sin32_ref, o_ref = rest
    else:
        (o_ref,) = rest
    x = x_ref[0]
    ms = jnp.mean(x * x, axis=-1, keepdims=True)
    h = (x * lax.rsqrt(ms + EPS) * g_ref[...]) * (1.0 + sc_ref[0]) + sh_ref[0]
    hb = h.astype(BF16)
    for c in range(QKV_WIDTH // 256):
        acc = jnp.dot(hb, w_ref[:, c * 256:(c + 1) * 256], preferred_element_type=F32)
        for half in range(2):
            j = 2 * c + half
            y = acc[:, half * LANES:(half + 1) * LANES]
            gain_row, rope_kind, scale = _BLOCK_KIND.get(j, (None, None, 1.0))
            if gain_row is not None:
                y = y * _head_rms(y) * gains_ref[gain_row:gain_row + 1, :]
            if rope and rope_kind is not None:
                cos_ref, sin_ref = (cos64_ref, sin64_ref) if rope_kind == 64 else (cos32_ref, sin32_ref)
                y = y * cos_ref[...] + _swap_pairs(y) * sin_ref[...]
            if scale != 1.0:
                y = y * scale
            o_ref[0, :, j * LANES:(j + 1) * LANES] = y.astype(BF16)


def _qkv_proj(x, sc, sh, g, w, gains, rope_tabs, tn=256):
    b, n, _ = x.shape
    rope = rope_tabs is not None
    in_specs = [pl.BlockSpec((1, tn, D_MODEL), lambda bi, i: (bi, i, 0)),
                pl.BlockSpec((1, 1, D_MODEL), lambda bi, i: (bi, 0, 0)),
                pl.BlockSpec((1, 1, D_MODEL), lambda bi, i: (bi, 0, 0)),
                pl.BlockSpec((1, D_MODEL), lambda bi, i: (0, 0)),
                pl.BlockSpec((D_MODEL, QKV_WIDTH), lambda bi, i: (0, 0)),
                pl.BlockSpec((2, LANES), lambda bi, i: (0, 0))]
    args = [x, sc, sh, g, w, gains]
    if rope:
        in_specs += [pl.BlockSpec((tn, LANES), lambda bi, i: (i, 0))] * 4
        args += list(rope_tabs)
    return pl.pallas_call(
        functools.partial(_qkv_kernel, rope=rope),
        grid=(b, n // tn),
        in_specs=in_specs,
        out_specs=pl.BlockSpec((1, tn, QKV_WIDTH), lambda bi, i: (bi, i, 0)),
        out_shape=jax.ShapeDtypeStruct((b, n, QKV_WIDTH), BF16),
        compiler_params=pltpu.CompilerParams(vmem_limit_bytes=VMEM_LIMIT),
        name="qkv_rope" if rope else "qkv_ctx",
    )(*args)


def _rope_tables(n):
    t = jnp.arange(n, dtype=jnp.int32)
    row = (t // GRID_W).astype(F32)
    col = (t % GRID_W).astype(F32)
    out = []
    for rot_dim in (HEAD_DIM, D_HALF):
        nf = rot_dim // 4
        inv = ROPE_BASE ** (-jnp.arange(nf, dtype=F32) / nf)
        ang = jnp.concatenate([row[:, None] * inv, col[:, None] * inv], axis=-1)
        cos = jnp.repeat(jnp.cos(ang), 2, axis=-1)
        sin = jnp.repeat(jnp.sin(ang), 2, axis=-1)
        sign = jnp.asarray(np.tile(np.array([-1.0, 1.0], np.float32), rot_dim // 2))
        reps = LANES // rot_dim
        out += [jnp.tile(cos, (1, reps)), jnp.tile(sin * sign, (1, reps))]
    return tuple(out)


def _ctx_attn_kernel(sink_ref, lam_ref, qkv_ref, subg_ref, o_ref, *, post_scale):
    def blk(j):
        return qkv_ref[0, :, j * LANES:(j + 1) * LANES]

    def two_head(qj, kj, vj, sinks):
        q, k, v = blk(qj), blk(kj), blk(vj)
        outs = []
        for mi in range(2):
            s = _qk(_mask_q(q, mi * HEAD_DIM, (mi + 1) * HEAD_DIM), k)
            outs.append(_joint_softmax_pv([s], [v], None if sinks is None else sinks[mi]))
        return jnp.where(_lane(outs[0].shape) < HEAD_DIM, outs[0], outs[1])

    for b2 in range(2):
        o_ref[0, :, b2 * LANES:(b2 + 1) * LANES] = two_head(b2, 2 + b2, 4 + b2, None).astype(BF16)
        o_ref[0, :, (2 + b2) * LANES:(3 + b2) * LANES] = two_head(
            6 + b2, 8, 9, (sink_ref[b2] * LOG2E, sink_ref[b2 + 2] * LOG2E)).astype(BF16)
        o_ref[0, :, (4 + b2) * LANES:(5 + b2) * LANES] = two_head(10 + b2, 12, 13, None).astype(BF16)
        q, k, v = blk(14 + b2), blk(16 + b2), blk(18 + b2)
        a = [_joint_softmax_pv([_qk(_mask_q(q, mi * D_HALF, (mi + 1) * D_HALF), k)], [v]) for mi in range(4)]
        o_ref[0, :, (6 + b2) * LANES:(7 + b2) * LANES] = _diff_finish(
            a[0], a[1], a[2], a[3], lam_ref[0], subg_ref[...], post_scale).astype(BF16)


def _ctx_attention(qkv, sink, lam, subg, post_scale):
    b, n, _ = qkv.shape
    return pl.pallas_call(
        functools.partial(_ctx_attn_kernel, post_scale=post_scale),
        grid=(b,),
        in_specs=[pl.BlockSpec(memory_space=pltpu.SMEM),
                  pl.BlockSpec(memory_space=pltpu.SMEM),
                  pl.BlockSpec((1, n, QKV_WIDTH), lambda bi: (bi, 0, 0)),
                  pl.BlockSpec((1, LANES), lambda bi: (0, 0))],
        out_specs=pl.BlockSpec((1, n, MIX_WIDTH), lambda bi: (bi, 0, 0)),
        out_shape=jax.ShapeDtypeStruct((b, n, MIX_WIDTH), BF16),
        compiler_params=pltpu.CompilerParams(vmem_limit_bytes=VMEM_LIMIT),
        name="ctx_attention",
    )(sink, lam, qkv, subg)


_NA_QROWS = 2
_NA_KBLOCKS = 5
_NA_CASES = (0, 2, 4, 60, 62)


def _natten_bias(rpb, rows):
    assert rows == 64
    nblk = rows // _NA_QROWS
    r0s = np.array(_NA_CASES)
    nq, nk = _NA_QROWS * GRID_W, _NA_KBLOCKS * LANES
    qr = r0s[:, None] + (np.arange(nq) // GRID_W)[None, :]
    qc = np.arange(nq) % GRID_W
    start = np.clip((r0s - NA_ROWS // 2) // 2, 0, nblk - _NA_KBLOCKS)
    kr = 2 * start[:, None] + (np.arange(nk) // GRID_W)[None, :]
    kc = np.arange(nk) % GRID_W
    rs = np.clip(qr - NA_ROWS // 2, 0, rows - NA_ROWS)
    vrow = (kr[:, None, :] >= rs[:, :, None]) & (kr[:, None, :] < rs[:, :, None] + NA_ROWS)
    cs = np.clip(qc - NA_COLS // 2, 0, GRID_W - NA_COLS)
    vcol = (kc[None, :] >= cs[:, None]) & (kc[None, :] < cs[:, None] + NA_COLS)
    valid = vrow & vcol[None]
    w = GRID_W
    nh, na, nc = rpb.shape
    front = (w - 1) - (NA_COLS - 1)
    line = jnp.pad(rpb.astype(F32), ((0, 0), (0, 0), (front, 2 * w - front - nc)))
    toep = jnp.tile(line, (1, 1, w))[:, :, :w * (2 * w - 1)].reshape(nh, na, w, 2 * w - 1)[..., w - 1:]
    qr_b = r0s[:, None] + np.arange(_NA_QROWS)[None, :]
    kr_b = 2 * start[:, None] + np.arange(nk // w)[None, :]
    a_idx = np.clip(kr_b[:, None, :] - qr_b[:, :, None] + (NA_ROWS - 1), 0, na - 1)
    bias = toep[:, a_idx]
    bias = bias.transpose(0, 1, 2, 4, 3, 5).reshape(nh, len(_NA_CASES), nq, nk)
    return jnp.where(valid[None], bias * LOG2E, NEG_INF)


def _natten_kernel(q_ref, k0, k1, k2, k3, k4, v0, v1, v2, v3, v4, kc_ref, vc_ref, bias_ref, o_ref):
    q = q_ref[0]
    k = jnp.concatenate([r[0] for r in (k0, k1, k2, k3, k4)], axis=0)
    v = jnp.concatenate([r[0] for r in (v0, v1, v2, v3, v4)], axis=0)
    kc = kc_ref[0, 0].astype(BF16)
    vc = vc_ref[0, 0].astype(BF16)
    outs = []
    for mi in range(2):
        qm = _mask_q(q, mi * HEAD_DIM, (mi + 1) * HEAD_DIM)
        s_loc = _qk(qm, k) + bias_ref[mi, 0]
        s_ctx = _qk(qm, kc)
        outs.append(_joint_softmax_pv([s_ctx, s_loc], [vc, v]))
    o_ref[0] = jnp.where(_lane(outs[0].shape) < HEAD_DIM, outs[0], outs[1]).astype(BF16)


def _natten(qkv, cache_k, cache_v, layer, bias):
    b, n, _ = qkv.shape
    nblk = n // LANES

    def start(i):
        return jnp.clip(i - 2, 0, nblk - _NA_KBLOCKS)

    def case(i):
        return jnp.where(i == 0, 0, jnp.where(i == 1, 1, jnp.where(i == nblk - 2, 3, jnp.where(i == nblk - 1, 4, 2))))

    def kv_spec(colblk, jj):
        return pl.BlockSpec((1, LANES, LANES), lambda bi, j, i: (bi, start(i) + jj, colblk + j))

    in_specs = ([pl.BlockSpec((1, LANES, LANES), lambda bi, j, i: (bi, i, j))]
                + [kv_spec(2, jj) for jj in range(_NA_KBLOCKS)]
                + [kv_spec(4, jj) for jj in range(_NA_KBLOCKS)]
                + [pl.BlockSpec((1, 1, cache_k.shape[2], LANES), lambda bi, j, i: (bi, layer, 0, j))] * 2
                + [pl.BlockSpec((2, 1, LANES, _NA_KBLOCKS * LANES), lambda bi, j, i: (j, case(i), 0, 0))])
    return pl.pallas_call(
        _natten_kernel,
        grid=(b, 2, nblk),
        in_specs=in_specs,
        out_specs=pl.BlockSpec((1, LANES, LANES), lambda bi, j, i: (bi, i, j)),
        out_shape=jax.ShapeDtypeStruct((b, n, 2 * LANES), BF16),
        compiler_params=pltpu.CompilerParams(vmem_limit_bytes=VMEM_LIMIT),
        name="natten",
    )(qkv, *([qkv] * (2 * _NA_KBLOCKS)), cache_k, cache_v, bias)


def _window_kernel(sink_ref, q_ref, k0, k1, k2, v0, v1, v2, kc_ref, vc_ref, o_ref, *, n):
    j = pl.program_id(1)
    i = pl.program_id(2)
    q = q_ref[0]
    k = jnp.concatenate([r[0] for r in (k0, k1, k2)], axis=0)
    v = jnp.concatenate([r[0] for r in (v0, v1, v2)], axis=0)
    kc = kc_ref[0, 0].astype(BF16)
    vc = vc_ref[0, 0].astype(BF16)
    r = lax.broadcasted_iota(jnp.int32, (LANES, 3 * LANES), 0)
    c = lax.broadcasted_iota(jnp.int32, (LANES, 3 * LANES), 1)
    kp = (i - 1) * LANES + c
    ok = (kp >= 0) & (kp < n) & (jnp.abs(c - LANES - r) <= WINDOW)
    outs = []
    for mi in range(2):
        qm = _mask_q(q, mi * HEAD_DIM, (mi + 1) * HEAD_DIM)
        s_loc = jnp.where(ok, _qk(qm, k), NEG_INF)
        s_ctx = _qk(qm, kc)
        outs.append(_joint_softmax_pv([s_ctx, s_loc], [vc, v], sink_ref[j + 2 * mi] * LOG2E))
    o_ref[0] = jnp.where(_lane(outs[0].shape) < HEAD_DIM, outs[0], outs[1]).astype(BF16)


def _window(qkv, cache_k, cache_v, layer, sink):
    b, n, _ = qkv.shape
    nblk = n // LANES

    def kv_spec(colblk, jj):
        return pl.BlockSpec((1, LANES, LANES), lambda bi, j, i: (bi, jnp.clip(i - 1 + jj, 0, nblk - 1), colblk))

    in_specs = ([pl.BlockSpec(memory_space=pltpu.SMEM),
                 pl.BlockSpec((1, LANES, LANES), lambda bi, j, i: (bi, i, 6 + j))]
                + [kv_spec(8, jj) for jj in range(3)]
                + [kv_spec(9, jj) for jj in range(3)]
                + [pl.BlockSpec((1, 1, cache_k.shape[2], LANES), lambda bi, j, i: (bi, layer, 0, 2))] * 2)
    return pl.pallas_call(
        functools.partial(_window_kernel, n=n),
        grid=(b, 2, nblk),
        in_specs=in_specs,
        out_specs=pl.BlockSpec((1, LANES, LANES), lambda bi, j, i: (bi, i, j)),
        out_shape=jax.ShapeDtypeStruct((b, n, 2 * LANES), BF16),
        compiler_params=pltpu.CompilerParams(vmem_limit_bytes=VMEM_LIMIT),
        name="window",
    )(sink, qkv, *([qkv] * 6), cache_k, cache_v)


_DENSE_TK = 512
_DENSE_STRIP = 128


def _dense_kernel(lam_ref, q_ref, k_ref, v_ref, kc_ref, vc_ref, subg_ref, o_ref,
                  qs_ref, kcb, vcb, s0, s1, p0, p1, l0, l1, *, n_maps, tq, diff, post_scale):
    n = k_ref.shape[1]
    nq = n // tq
    rows = n_maps * tq
    width = LANES // n_maps
    n_chunks = n // _DENSE_TK
    kcb[...] = kc_ref[0, 0].astype(BF16)
    vcb[...] = vc_ref[0, 0].astype(BF16)
    n_ctx = kcb.shape[0]

    def scores(i, s_ref):
        q = q_ref[0, pl.ds(pl.multiple_of(i * tq, tq), tq), :]
        for mi in range(n_maps):
            qs_ref[mi * tq:(mi + 1) * tq, :] = _mask_q(q, mi * width, (mi + 1) * width)
        qs = qs_ref[...]
        for c in range(n_chunks):
            s_ref[:, c * _DENSE_TK:(c + 1) * _DENSE_TK] = _qk(qs, k_ref[0, c * _DENSE_TK:(c + 1) * _DENSE_TK, :])
        s_ref[:, n:] = _qk(qs, kcb[...])

    def probs(s_ref, p_ref, l_ref):
        n_tiles = (n + n_ctx) // LANES
        for r0 in range(0, rows, _DENSE_STRIP):
            mx = None
            for t in range(n_tiles):
                piece = s_ref[r0:r0 + _DENSE_STRIP, t * LANES:(t + 1) * LANES]
                mx = piece if mx is None else jnp.maximum(mx, piece)
            m = jnp.broadcast_to(mx.max(axis=-1, keepdims=True), (_DENSE_STRIP, LANES))
            ls = None
            for t in range(n_tiles):
                p = jnp.exp2(s_ref[r0:r0 + _DENSE_STRIP, t * LANES:(t + 1) * LANES] - m)
                ls = p if ls is None else ls + p
                p_ref[r0:r0 + _DENSE_STRIP, t * LANES:(t + 1) * LANES] = p.astype(BF16)
            l_ref[r0:r0 + _DENSE_STRIP, :] = ls

    def values(i, p_ref, l_ref):
        o = (jnp.dot(p_ref[:, :n], v_ref[0], preferred_element_type=F32)
             + jnp.dot(p_ref[:, n:], vcb[...], preferred_element_type=F32))
        a = o / l_ref[...].sum(axis=-1, keepdims=True)
        parts = [a[mi * tq:(mi + 1) * tq, :] for mi in range(n_maps)]
        if diff:
            out = _diff_finish(parts[0], parts[1], parts[2], parts[3], lam_ref[0], subg_ref[...], post_scale)
        else:
            out = jnp.where(_lane(parts[0].shape) < HEAD_DIM, parts[0], parts[1])
        o_ref[0, pl.ds(pl.multiple_of(i * tq, tq), tq), :] = out.astype(BF16)

    scores(0, s0)
    scores(1, s1)
    probs(s0, p0, l0)

    def pair(t, carry):
        i = 2 * t
        scores(i, s0)
        probs(s1, p1, l1)
        values(i - 2, p0, l0)
        scores(i + 1, s1)
        probs(s0, p0, l0)
        values(i - 1, p1, l1)
        return carry

    lax.fori_loop(1, nq // 2, pair, 0)
    probs(s1, p1, l1)
    values(nq - 2, p0, l0)
    values(nq - 1, p1, l1)


def _dense(qkv, cache_k, cache_v, layer, lam, subg, *, diff, post_scale, rows=512):
    b, n, _ = qkv.shape
    n_maps = 4 if diff else 2
    tq = rows // n_maps
    assert n % (2 * tq) == 0 and n % _DENSE_TK == 0
    if diff:
        qb, kb, vb, cb = 14, 16, 18, 4
        kmap = lambda bi, j: (bi, 0, kb + j)
        vmap = lambda bi, j: (bi, 0, vb + j)
        cmap = lambda bi, j: (bi, layer, 0, cb + j)
    else:
        qb, kb, vb, cb = 10, 12, 13, 3
        kmap = lambda bi, j: (bi, 0, kb)
        vmap = lambda bi, j: (bi, 0, vb)
        cmap = lambda bi, j: (bi, layer, 0, cb)
    m = cache_k.shape[2]
    keys = n + m
    return pl.pallas_call(
        functools.partial(_dense_kernel, n_maps=n_maps, tq=tq, diff=diff, post_scale=post_scale),
        grid=(b, 2),
        in_specs=[pl.BlockSpec(memory_space=pltpu.SMEM),
                  pl.BlockSpec((1, n, LANES), lambda bi, j: (bi, 0, qb + j)),
                  pl.BlockSpec((1, n, LANES), kmap),
                  pl.BlockSpec((1, n, LANES), vmap),
                  pl.BlockSpec((1, 1, m, LANES), cmap),
                  pl.BlockSpec((1, 1, m, LANES), cmap),
                  pl.BlockSpec((1, LANES), lambda bi, j: (0, 0))],
        out_specs=pl.BlockSpec((1, n, LANES), lambda bi, j: (bi, 0, j)),
        out_shape=jax.ShapeDtypeStruct((b, n, 2 * LANES), BF16),
        scratch_shapes=[pltpu.VMEM((rows, LANES), BF16), pltpu.VMEM((m, LANES), BF16), pltpu.VMEM((m, LANES), BF16),
                        pltpu.VMEM((rows, keys), F32), pltpu.VMEM((rows, keys), F32),
                        pltpu.VMEM((rows, keys), BF16), pltpu.VMEM((rows, keys), BF16),
                        pltpu.VMEM((rows, LANES), F32), pltpu.VMEM((rows, LANES), F32)],
        compiler_params=pltpu.CompilerParams(vmem_limit_bytes=56 * 1024 * 1024),
        name="diff_attn" if diff else "dense_attn",
    )(lam, qkv, qkv, qkv, cache_k, cache_v, subg)


_ROUTE_E1, _ROUTE_E2, _ROUTE_W1, _ROUTE_W2 = 0, 1, 2, 3


def _oproj_kernel(*refs, widths):
    nm = len(widths)
    mix_refs = refs[:nm]
    x_ref, g1_ref, sc_ref, sh_ref, g_ref, wo_ref, wr_ref, xo_ref, h_ref, route_ref = refs[nm:]
    acc = None
    off = 0
    for mref, wd in zip(mix_refs, widths):
        part = jnp.dot(mref[0], wo_ref[off:off + wd, :], preferred_element_type=F32)
        acc = part if acc is None else acc + part
        off += wd
    x = x_ref[0] + g1_ref[0] * acc
    xo_ref[0] = x
    ms = jnp.mean(x * x, axis=-1, keepdims=True)
    h = (x * lax.rsqrt(ms + EPS) * g_ref[...]) * (1.0 + sc_ref[0]) + sh_ref[0]
    h_ref[0] = h
    logits = jnp.dot(h, wr_ref[...], preferred_element_type=F32, precision=lax.Precision.HIGHEST)
    lane = _lane(logits.shape).astype(F32)
    big = float(LANES)
    glog = jnp.where(lane < N_GROUPS, logits, -jnp.inf)
    gmax = glog.max(axis=-1, keepdims=True)
    gsel = jnp.where(glog == gmax, lane, big).min(axis=-1, keepdims=True)
    pg = 1.0 / jnp.exp(glog - gmax).sum(axis=-1, keepdims=True)
    e_lo = N_GROUPS + EXPERTS_PER_GROUP * gsel
    el = jnp.where((lane >= e_lo) & (lane < e_lo + EXPERTS_PER_GROUP), logits, -jnp.inf)
    v1 = el.max(axis=-1, keepdims=True)
    i1 = jnp.where(el == v1, lane, big).min(axis=-1, keepdims=True)
    el2 = jnp.where(lane == i1, -jnp.inf, el)
    v2 = el2.max(axis=-1, keepdims=True)
    i2 = jnp.where(el2 == v2, lane, big).min(axis=-1, keepdims=True)
    t = jnp.exp(v2 - v1)
    w1 = pg / (1.0 + t)
    w2 = pg * t / (1.0 + t)
    rec = jnp.where(lane == _ROUTE_E1, i1 - N_GROUPS,
                    jnp.where(lane == _ROUTE_E2, i2 - N_GROUPS,
                              jnp.where(lane == _ROUTE_W1, w1, jnp.where(lane == _ROUTE_W2, w2, 0.0))))
    route_ref[0] = rec


def _oproj_router(mixes, x, g1, sc2, sh2, g, wo, wr, tn=256):
    b, n, _ = x.shape
    widths = tuple(m.shape[-1] for m in mixes)
    row = lambda bi, i: (bi, i, 0)
    per_b = lambda bi, i: (bi, 0, 0)
    const = lambda bi, i: (0, 0)
    in_specs = ([pl.BlockSpec((1, tn, wd), row) for wd in widths]
                + [pl.BlockSpec((1, tn, D_MODEL), row)]
                + [pl.BlockSpec((1, 1, D_MODEL), per_b)] * 3
                + [pl.BlockSpec((1, D_MODEL), const),
                   pl.BlockSpec((MIX_WIDTH, D_MODEL), const),
                   pl.BlockSpec((D_MODEL, LANES), const)])
    return pl.pallas_call(
        functools.partial(_oproj_kernel, widths=widths),
        grid=(b, n // tn),
        in_specs=in_specs,
        out_specs=[pl.BlockSpec((1, tn, D_MODEL), row), pl.BlockSpec((1, tn, D_MODEL), row),
                   pl.BlockSpec((1, tn, LANES), row)],
        out_shape=[jax.ShapeDtypeStruct((b, n, D_MODEL), F32), jax.ShapeDtypeStruct((b, n, D_MODEL), F32),
                   jax.ShapeDtypeStruct((b, n, LANES), F32)],
        compiler_params=pltpu.CompilerParams(vmem_limit_bytes=VMEM_LIMIT),
        name="oproj_router",
    )(*mixes, x, g1, sc2, sh2, g, wo, wr)


_RANK_BLOCK = 512


def _dispatch_plan(route):
    t = route.shape[0]
    e = route[:, :2].astype(jnp.int32).reshape(-1)
    nb = (2 * t) // _RANK_BLOCK
    onehot = (e[:, None] == jnp.arange(N_EXPERTS)[None, :]).reshape(nb, _RANK_BLOCK, N_EXPERTS)
    tri = jnp.asarray(np.tril(np.ones((_RANK_BLOCK, _RANK_BLOCK), np.float32), -1), BF16)
    within = jnp.einsum("ij,bjk->bik", tri, onehot.astype(BF16), preferred_element_type=F32)
    oh = onehot.astype(F32)
    block_counts = jnp.sum(oh, axis=1)
    block_off = jnp.cumsum(block_counts, axis=0) - block_counts
    counts = jnp.sum(block_counts, axis=0).astype(jnp.int32)
    padded = ((counts + MOE_TILE - 1) // MOE_TILE) * MOE_TILE
    ends = jnp.cumsum(padded)
    starts = (ends - padded).astype(F32)
    slot = jnp.sum(oh * (within + block_off[:, None, :] + starts[None, None, :]), axis=-1)
    n_tiles = (2 * t) // MOE_TILE + N_EXPERTS
    tile_id = jnp.arange(n_tiles, dtype=jnp.int32)
    used = ends[-1] // MOE_TILE
    last = jnp.minimum(tile_id, used - 1)
    te = jnp.sum(((ends // MOE_TILE)[None, :] <= last[:, None]).astype(jnp.int32), axis=1)
    valid = (tile_id < used).astype(jnp.int32)
    return te, valid, slot.reshape(t, 2).astype(jnp.int32)


def _dispatch_kernel(slot_ref, h_ref, init_hbm, hs_hbm, hbuf, sems, *, tn, n_steps):
    del init_hbm
    g = pl.program_id(0)
    par = g % 2
    hbuf[par] = h_ref[...]

    def issue(r, carry):
        for k in range(2):
            s = slot_ref[0, 0, 2 * r + k]
            pltpu.make_async_copy(hbuf.at[par, pl.ds(r, 1), :], hs_hbm.at[pl.ds(s, 1), :], sems.at[par]).start()
        return carry

    lax.fori_loop(0, tn, issue, 0, unroll=4)

    def wait_step(p):
        for _ in range(2):
            pltpu.make_async_copy(hbuf.at[p], hs_hbm.at[pl.ds(0, tn), :], sems.at[p]).wait()

    @pl.when(g > 0)
    def _():
        wait_step(1 - par)

    @pl.when(g == n_steps - 1)
    def _():
        wait_step(par)


def _dispatch_rows(slots, h, n_slots, tn=256):
    t = h.shape[0]
    n_steps = t // tn
    return pl.pallas_call(
        functools.partial(_dispatch_kernel, tn=tn, n_steps=n_steps),
        grid=(n_steps,),
        in_specs=[pl.BlockSpec((1, 1, 2 * tn), lambda g: (g, 0, 0), memory_space=pltpu.SMEM),
                  pl.BlockSpec((tn, D_MODEL), lambda g: (g, 0)),
                  pl.BlockSpec(memory_space=pl.ANY)],
        out_specs=pl.BlockSpec(memory_space=pl.ANY),
        out_shape=jax.ShapeDtypeStruct((n_slots, D_MODEL), F32),
        input_output_aliases={2: 0},
        scratch_shapes=[pltpu.VMEM((2, tn, D_MODEL), F32), pltpu.SemaphoreType.DMA((2,))],
        compiler_params=pltpu.CompilerParams(dimension_semantics=("arbitrary",)),
        name="moe_dispatch",
    )(slots.reshape(n_steps, 1, 2 * tn), h, jnp.zeros((n_slots, D_MODEL), F32))


def _expert_kernel(te_ref, valid_ref, hs_ref, wg_ref, wu_ref, wd_ref, o_ref, wgb, wub, wdb):
    i = pl.program_id(0)

    @pl.when(valid_ref[i] == 1)
    def _():
        @pl.when((i == 0) | (te_ref[i] != te_ref[jnp.maximum(i - 1, 0)]))
        def _():
            wgb[...] = wg_ref[0, 0].astype(BF16)
            wub[...] = wu_ref[0, 0].astype(BF16)
            wdb[...] = wd_ref[0, 0].astype(BF16)

        hb = hs_ref[...].astype(BF16)
        gate = jnp.dot(hb, wgb[...], preferred_element_type=F32)
        up = jnp.dot(hb, wub[...], preferred_element_type=F32)
        hdn = gate * (1.0 / (1.0 + jnp.exp(-gate))) * up
        o_ref[...] = jnp.dot(hdn.astype(BF16), wdb[...], preferred_element_type=F32)

    @pl.when(valid_ref[i] == 0)
    def _():
        o_ref[...] = jnp.zeros_like(o_ref)


def _expert_tiles(te, valid, hs, w_gate, w_up, w_down, layer):
    n_tiles = hs.shape[0] // MOE_TILE
    grid_spec = pltpu.PrefetchScalarGridSpec(
        num_scalar_prefetch=2,
        grid=(n_tiles,),
        in_specs=[pl.BlockSpec((MOE_TILE, D_MODEL), lambda i, te, va: (i, 0)),
                  pl.BlockSpec((1, 1, D_MODEL, EXPERT_FF), lambda i, te, va: (layer, te[i], 0, 0)),
                  pl.BlockSpec((1, 1, D_MODEL, EXPERT_FF), lambda i, te, va: (layer, te[i], 0, 0)),
                  pl.BlockSpec((1, 1, EXPERT_FF, D_MODEL), lambda i, te, va: (layer, te[i], 0, 0))],
        out_specs=pl.BlockSpec((MOE_TILE, D_MODEL), lambda i, te, va: (i, 0)),
        scratch_shapes=[pltpu.VMEM((D_MODEL, EXPERT_FF), BF16), pltpu.VMEM((D_MODEL, EXPERT_FF), BF16),
                        pltpu.VMEM((EXPERT_FF, D_MODEL), BF16)])
    return pl.pallas_call(
        _expert_kernel,
        grid_spec=grid_spec,
        out_shape=jax.ShapeDtypeStruct((n_tiles * MOE_TILE, D_MODEL), F32),
        compiler_params=pltpu.CompilerParams(vmem_limit_bytes=VMEM_LIMIT, dimension_semantics=("arbitrary",)),
        name="moe_experts",
    )(te, valid, hs, w_gate, w_up, w_down)


def _combine_kernel(slot_ref, next_slot_ref, x_ref, g2_ref, route_ref, fg_ref, o_hbm, y_ref, obuf, sems,
                    *, tn, n_steps, final):
    g = pl.program_id(0)
    par = g % 2

    def issue(idx_ref, p):
        def body(r, carry):
            for k in range(2):
                s = idx_ref[0, 0, 2 * r + k]
                pltpu.make_async_copy(o_hbm.at[pl.ds(s, 1), :], obuf.at[p, k, pl.ds(r, 1), :], sems.at[p]).start()
            return carry

        lax.fori_loop(0, tn, body, 0, unroll=4)

    @pl.when(g == 0)
    def _():
        issue(slot_ref, 0)

    @pl.when(g + 1 < n_steps)
    def _():
        issue(next_slot_ref, 1 - par)

    for k in range(2):
        pltpu.make_async_copy(o_hbm.at[pl.ds(0, tn), :], obuf.at[par, k], sems.at[par]).wait()
    rec = route_ref[0]
    w1 = rec[:, _ROUTE_W1:_ROUTE_W1 + 1]
    w2 = rec[:, _ROUTE_W2:_ROUTE_W2 + 1]
    y = x_ref[0] + g2_ref[0] * (w1 * obuf[par, 0] + w2 * obuf[par, 1])
    if final:
        ms = jnp.mean(y * y, axis=-1, keepdims=True)
        y = y * lax.rsqrt(ms + EPS) * fg_ref[...]
    y_ref[0] = y


def _combine(slots, x, g2, route, final_g, o, *, final, tn=256):
    b, n, _ = x.shape
    nt = n // tn
    n_steps = b * nt
    slots3 = slots.reshape(n_steps, 1, 2 * tn)
    row = lambda g: (g // nt, g % nt, 0)
    return pl.pallas_call(
        functools.partial(_combine_kernel, tn=tn, n_steps=n_steps, final=final),
        grid=(n_steps,),
        in_specs=[pl.BlockSpec((1, 1, 2 * tn), lambda g: (g, 0, 0), memory_space=pltpu.SMEM),
                  pl.BlockSpec((1, 1, 2 * tn), lambda g: (jnp.minimum(g + 1, n_steps - 1), 0, 0),
                               memory_space=pltpu.SMEM),
                  pl.BlockSpec((1, tn, D_MODEL), row),
                  pl.BlockSpec((1, 1, D_MODEL), lambda g: (g // nt, 0, 0)),
                  pl.BlockSpec((1, tn, LANES), row),
                  pl.BlockSpec((1, D_MODEL), lambda g: (0, 0)),
                  pl.BlockSpec(memory_space=pl.ANY)],
        out_specs=pl.BlockSpec((1, tn, D_MODEL), row),
        out_shape=jax.ShapeDtypeStruct((b, n, D_MODEL), F32),
        scratch_shapes=[pltpu.VMEM((2, 2, tn, D_MODEL), F32), pltpu.SemaphoreType.DMA((2,))],
        compiler_params=pltpu.CompilerParams(vmem_limit_bytes=VMEM_LIMIT, dimension_semantics=("arbitrary",)),
        name="moe_combine",
    )(slots3, slots3, x, g2, route, final_g, o)


def _moe(x, h, route, g2, final_g, w_gate, w_up, w_down, layer, *, final):
    b, n, _ = x.shape
    te, valid, slots = _dispatch_plan(route.reshape(b * n, LANES))
    hs = _dispatch_rows(slots, h.reshape(b * n, D_MODEL), te.shape[0] * MOE_TILE)
    o = _expert_tiles(te, valid, hs, w_gate, w_up, w_down, layer)
    return _combine(slots, x, g2, route, final_g, o, final=final)


def _layer(x, mod, layer, prm, *, latent, cache=None, rope_tabs=None, final):
    sh1, sc1, g1, sh2, sc2, g2 = mod
    qkv = _qkv_proj(x, sc1, sh1, prm["norm1_g"], prm["w_in"], prm["gains"], rope_tabs if latent else None)
    if latent:
        cache_k, cache_v = cache
        mixes = [_natten(qkv, cache_k, cache_v, layer, prm["na_bias"]),
                 _window(qkv, cache_k, cache_v, layer, prm["sink"]),
                 _dense(qkv, cache_k, cache_v, layer, prm["lam"], prm["subg"], diff=False, post_scale=1.0),
                 _dense(qkv, cache_k, cache_v, layer, prm["lam"], prm["subg"], diff=True,
                        post_scale=prm["post_scale"])]
    else:
        mixes = [_ctx_attention(qkv, prm["sink"], prm["lam"], prm["subg"], prm["post_scale"])]
    x1, h2, route = _oproj_router(mixes, x, g1, sc2, sh2, prm["norm2_g"], prm["w_out"], prm["w_router"])
    y = _moe(x1, h2, route, g2, prm["final_g"], prm["w_gate"], prm["w_up"], prm["w_down"], layer, final=final)
    return y, qkv


def kernel(x_prompt, x_sample, cache_k, cache_v, c, c_ctx, w_ada, b_ada, norm1_g, norm2_g, w_in, w_out, na_rpb,
           sink_logit, qn_g, kn_g, lam_q1, lam_k1, lam_q2, lam_k2, subln_g, w_router_group, w_router_expert,
           w_gate, w_up, w_down, final_g):
    nb, n_ctx, _ = x_prompt.shape
    db, n_lat, _ = x_sample.shape
    past = cache_k.shape[2]
    cond = jnp.concatenate([c_ctx[None, :], c, jnp.zeros((16 - 1 - db, D_MODEL), F32)], axis=0)
    mod = _modulation(cond, w_ada, b_ada)
    ck = cache_k.reshape(db, DEPTH, past, CACHE_WIDTH)
    cv = cache_v.reshape(db, DEPTH, past, CACHE_WIDTH)
    rope_tabs = _rope_tables(n_lat)
    xp, xs = x_prompt, x_sample
    new_k, new_v = [], []
    for l in range(DEPTH):
        lam_init = 0.8 - 0.6 * math.exp(-0.3 * l)
        lam = (jnp.exp(jnp.sum(lam_q1[l] * lam_k1[l])) - jnp.exp(jnp.sum(lam_q2[l] * lam_k2[l])) + lam_init)
        prm = {
            "norm1_g": norm1_g[l][None, :], "norm2_g": norm2_g[l][None, :], "final_g": final_g[None, :],
            "w_in": _permute_w_in(w_in[l]).astype(BF16),
            "w_out": _permute_w_out(w_out[l]).astype(BF16),
            "gains": jnp.stack([jnp.tile(qn_g[l], 2), jnp.tile(kn_g[l], 2)]),
            "na_bias": _natten_bias(na_rpb[l], n_lat // GRID_W),
            "sink": sink_logit[l],
            "lam": lam.reshape(1).astype(F32),
            "subg": jnp.tile(subln_g[l], 2)[None, :],
            "post_scale": 1.0 - lam_init,
            "w_router": jnp.concatenate(
                [w_router_group[l], w_router_expert[l],
                 jnp.zeros((D_MODEL, LANES - N_GROUPS - N_EXPERTS), F32)], axis=1),
            "w_gate": w_gate, "w_up": w_up, "w_down": w_down,
        }
        m = mod[l].reshape(16, 6, 1, D_MODEL)
        mod_ctx = [jnp.broadcast_to(m[0:1, s], (nb, 1, D_MODEL)) for s in range(6)]
        mod_lat = [m[1:1 + db, s] for s in range(6)]
        final = l == DEPTH - 1
        xp, qkv_c = _layer(xp, mod_ctx, l, prm, latent=False, final=final)
        xs, _ = _layer(xs, mod_lat, l, prm, latent=True, cache=(ck, cv), rope_tabs=rope_tabs, final=final)
        new_k.append(jnp.concatenate([qkv_c[..., 256:512], qkv_c[..., 1024:1152], qkv_c[..., 1536:1664],
                                      qkv_c[..., 2048:2304]], axis=-1))
        new_v.append(jnp.concatenate([qkv_c[..., 512:768], qkv_c[..., 1152:1280], qkv_c[..., 1664:1792],
                                      qkv_c[..., 2304:2560]], axis=-1))
    heads = CACHE_WIDTH // HEAD_DIM
    new_cache_k = jnp.stack(new_k, axis=1).astype(F32).reshape(nb, DEPTH, n_ctx, heads, HEAD_DIM)
    new_cache_v = jnp.stack(new_v, axis=1).astype(F32).reshape(nb, DEPTH, n_ctx, heads, HEAD_DIM)
    return (xp, xs, new_cache_k, new_cache_v)
```
